```python
import math
import jax, jax.numpy as jnp
from jax import lax
import numpy as np

D_MODEL = 1024
BATCH = 8
SEQ = 4096
DEPTH = 1

CHUNK = 64
D_MIX = D_MODEL
D_ATT = D_MIX // 2
N_ATT_HEADS = 8
ATT_HEAD_DIM = D_ATT // N_ATT_HEADS
LEFT_CHUNKS = 8
BAND = (LEFT_CHUNKS + 1) * CHUNK
REL_CLIP = 128
D_GLA_V = D_MIX - D_ATT
N_GLA_HEADS = 4
D_GLA_K = D_GLA_V // 2
GLA_HEAD_K = D_GLA_K // N_GLA_HEADS
GLA_HEAD_V = D_GLA_V // N_GLA_HEADS
GLA_LOW_RANK = 16
GLA_TAU = 16.0
D_PLE = 256
LN_EPS = 1e-5
RMS_EPS = 1e-6
DEEPNORM_ALPHA = (2.0 * DEPTH) ** 0.25
DEEPNORM_BETA = (8.0 * DEPTH) ** -0.25
D_IN_PROJ = 4 * D_ATT + 2 * D_GLA_K + 2 * D_GLA_V + GLA_LOW_RANK

kernel_name = "hymba_chunked_attn_gla_deepnorm"


def _split_sizes():
    return (D_ATT, D_ATT, D_ATT, D_ATT, D_GLA_K, D_GLA_K, D_GLA_V, D_GLA_V, GLA_LOW_RANK)


def layer_norm(x, g, b):
    xf = x.astype(jnp.float32)
    mu = jnp.mean(xf, axis=-1, keepdims=True)
    xc = xf - mu
    var = jnp.mean(xc * xc, axis=-1, keepdims=True)
    y = xc * lax.rsqrt(var + LN_EPS) * g.astype(jnp.float32) + b.astype(jnp.float32)
    return y.astype(x.dtype)


def chunk_band_attention(q, k, v, rel_table):
    b, s, h, dh = q.shape
    nc = s // CHUNK
    q = jnp.transpose(q, (0, 2, 1, 3))
    k = jnp.transpose(k, (0, 2, 1, 3))
    v = jnp.transpose(v, (0, 2, 1, 3))
    left = LEFT_CHUNKS * CHUNK
    k_pad = jnp.pad(k, ((0, 0), (0, 0), (left, 0), (0, 0)))
    v_pad = jnp.pad(v, ((0, 0), (0, 0), (left, 0), (0, 0)))
    q_chunks = jnp.moveaxis(q.reshape(b, h, nc, CHUNK, dh), 2, 0)
    qi = jnp.arange(CHUNK)[:, None]
    kj = jnp.arange(BAND)[None, :]
    rel = qi + left - kj
    rel_idx = jnp.clip(rel, -REL_CLIP, REL_CLIP) + REL_CLIP
    bias = rel_table.astype(jnp.float32)[:, rel_idx]
    scale = ATT_HEAD_DIM ** -0.5

    def one_chunk(args):
        c, q_blk = args
        start = c * CHUNK
        k_blk = lax.dynamic_slice_in_dim(k_pad, start, BAND, axis=2)
        v_blk = lax.dynamic_slice_in_dim(v_pad, start, BAND, axis=2)
        sc = jnp.einsum('bhqd,bhkd->bhqk', q_blk, k_blk).astype(jnp.float32) * scale + bias
        valid = (start - left + kj) >= 0
        sc = jnp.where(valid, sc, -1e30)
        pr = jax.nn.softmax(sc, axis=-1)
        return jnp.einsum('bhqk,bhkd->bhqd', pr.astype(v_blk.dtype), v_blk)

    out = lax.map(one_chunk, (jnp.arange(nc), q_chunks))
    out = jnp.transpose(out, (1, 0, 3, 2, 4))
    return out.reshape(b, s, h * dh)


def gla_chunked(q, k, v, log_a):
    b, s, h, dk = q.shape
    dv = v.shape[-1]
    nc = s // CHUNK
    f32 = jnp.float32
    qf = q.astype(f32).reshape(b, nc, CHUNK, h, dk) * (dk ** -0.5)
    kf = k.astype(f32).reshape(b, nc, CHUNK, h, dk)
    vf = v.astype(f32).reshape(b, nc, CHUNK, h, dv)
    L = jnp.cumsum(log_a.astype(f32).reshape(b, nc, CHUNK, h, dk), axis=2)
    L_end = L[:, :, -1:]
    eL = jnp.exp(L)
    e_negL = jnp.exp(-L)
    q_fwd = qf * eL
    a_causal = jnp.einsum('bnthd,bnshd->bnhts', q_fwd, kf * e_negL)
    a_anti = jnp.einsum('bnthd,bnshd->bnhts', qf * e_negL, kf * eL)
    tril = jnp.tril(jnp.ones((CHUNK, CHUNK), dtype=bool))
    att = jnp.where(tril, a_causal, a_anti)
    o_intra = jnp.einsum('bnhts,bnshv->bnthv', att, vf)
    kv = jnp.einsum('bnshd,bnshv->bnhdv', kf * jnp.exp(L_end - L), vf)
    chunk_decay = jnp.exp(L_end[:, :, 0])

    def step(state, inp):
        dec, kv_c = inp
        return dec[..., None] * state + kv_c, state

    init = jnp.zeros((b, h, dk, dv), f32)
    _, s_prev = lax.scan(step, init, (jnp.moveaxis(chunk_decay, 1, 0), jnp.moveaxis(kv, 1, 0)))
    s_prev = jnp.moveaxis(s_prev, 0, 1)
    o_inter = jnp.einsum('bnthd,bnhdv->bnthv', q_fwd, s_prev)
    return (o_intra + o_inter).reshape(b, s, h, dv)


def setup_inputs(seed: int = 0) -> dict:
    key = jax.random.key(seed)
    ks = jax.random.split(key, 20)
    f32 = jnp.float32
    n = jax.random.normal
    x = n(ks[0], (BATCH, SEQ, D_MODEL), f32)
    p = n(ks[1], (DEPTH, BATCH, SEQ, D_PLE), f32)
    ln_in_g = 1.0 + 0.02 * n(ks[2], (D_MODEL,), f32)
    ln_in_b = 0.02 * n(ks[3], (D_MODEL,), f32)
    w_in = n(ks[4], (DEPTH, D_MODEL, D_IN_PROJ), f32) * D_MODEL ** -0.5
    w_gla_gate = n(ks[5], (DEPTH, GLA_LOW_RANK, D_GLA_K), f32) * GLA_LOW_RANK ** -0.5
    b_gla_gate = 0.1 * n(ks[6], (DEPTH, D_GLA_K), f32) + 1.0
    rel_bias = 0.1 * n(ks[7], (DEPTH, N_ATT_HEADS, 2 * REL_CLIP + 1), f32)
    gla_norm_g = 1.0 + 0.02 * n(ks[8], (DEPTH, N_GLA_HEADS, GLA_HEAD_V), f32)
    w_out = n(ks[9], (DEPTH, D_MIX, D_MODEL), f32) * (D_MIX ** -0.5) * DEEPNORM_BETA
    w_ple = n(ks[10], (DEPTH, D_PLE, D_MODEL), f32) * D_PLE ** -0.5
    w_ple_gate = n(ks[11], (DEPTH, D_MODEL, D_MODEL), f32) * D_MODEL ** -0.5
    b_ple_gate = 0.02 * n(ks[12], (DEPTH, D_MODEL), f32)
    ln_g = 1.0 + 0.02 * n(ks[13], (DEPTH, D_MODEL), f32)
    ln_b = 0.02 * n(ks[14], (DEPTH, D_MODEL), f32)
    return {"x": x, "p": p, "ln_in_g": ln_in_g, "ln_in_b": ln_in_b, "w_in": w_in,
            "w_gla_gate": w_gla_gate, "b_gla_gate": b_gla_gate, "rel_bias": rel_bias,
            "gla_norm_g": gla_norm_g, "w_out": w_out, "w_ple": w_ple,
            "w_ple_gate": w_ple_gate, "b_ple_gate": b_ple_gate, "ln_g": ln_g, "ln_b": ln_b}


def reference(x, p, ln_in_g, ln_in_b, w_in, w_gla_gate, b_gla_gate, rel_bias,
              gla_norm_g, w_out, w_ple, w_ple_gate, b_ple_gate, ln_g, ln_b):
    b, s, _ = x.shape
    split_points = np.cumsum(np.array(_split_sizes()))[:-1].tolist()
    h = layer_norm(x, ln_in_g, ln_in_b)
    for i in range(DEPTH):
        proj = h @ w_in[i]
        (aq, ak, av, ag, gq, gk, gv, gg, glr) = jnp.split(proj, split_points, axis=-1)
        att = chunk_band_attention(
            aq.reshape(b, s, N_ATT_HEADS, ATT_HEAD_DIM),
            ak.reshape(b, s, N_ATT_HEADS, ATT_HEAD_DIM),
            av.reshape(b, s, N_ATT_HEADS, ATT_HEAD_DIM),
            rel_bias[i])
        att = att * jax.nn.silu(ag)
        gate_logit = (glr @ w_gla_gate[i] + b_gla_gate[i]).astype(jnp.float32)
        log_a = jax.nn.log_sigmoid(gate_logit) / GLA_TAU
        o = gla_chunked(
            gq.reshape(b, s, N_GLA_HEADS, GLA_HEAD_K),
            gk.reshape(b, s, N_GLA_HEADS, GLA_HEAD_K),
            gv.reshape(b, s, N_GLA_HEADS, GLA_HEAD_V),
            log_a.reshape(b, s, N_GLA_HEADS, GLA_HEAD_K))
        o = o * lax.rsqrt(jnp.mean(o * o, axis=-1, keepdims=True) + RMS_EPS) \
            * gla_norm_g[i].astype(jnp.float32)
        gla = o.reshape(b, s, D_GLA_V).astype(h.dtype) * jax.nn.silu(gg)
        mix = jnp.concatenate([att.astype(h.dtype), gla], axis=-1) @ w_out[i]
        r = DEEPNORM_ALPHA * h + mix
        ple_gate = jax.nn.sigmoid(r @ w_ple_gate[i] + b_ple_gate[i])
        r = r + ple_gate * (p[i] @ w_ple[i])
        h = layer_norm(r, ln_g[i], ln_b[i])
    return h
```

```python
import functools

import jax
import jax.numpy as jnp
from jax import lax
from jax.experimental import pallas as pl
from jax.experimental.pallas import tpu as pltpu

CHUNK = 64
N_ATT_HEADS = 8
ATT_HEAD_DIM = 64
D_ATT = N_ATT_HEADS * ATT_HEAD_DIM
LEFT_CHUNKS = 8
LEFT = LEFT_CHUNKS * CHUNK
BAND = LEFT + CHUNK
REL_CLIP = 128
N_GLA_HEADS = 4
GLA_HEAD_K = 64
GLA_HEAD_V = 128
D_GLA_K = N_GLA_HEADS * GLA_HEAD_K
D_GLA_V = N_GLA_HEADS * GLA_HEAD_V
GLA_LOW_RANK = 16
GLA_TAU = 16.0
LN_EPS = 1e-5
RMS_EPS = 1e-6
MASK_VALUE = -1e30

LANES = 128
HEAD_PAIR = 2 * ATT_HEAD_DIM
assert HEAD_PAIR == LANES and 2 * GLA_HEAD_K == LANES

_OFF_AQKV = 0
_OFF_AG = 3 * D_ATT
_OFF_GQK = 4 * D_ATT
_OFF_GV = _OFF_GQK + 2 * D_GLA_K
_OFF_GG = _OFF_GV + D_GLA_V
_OFF_LR = _OFF_GG + D_GLA_V
D_IN_PROJ = _OFF_LR + GLA_LOW_RANK
D_IN_PAD = _OFF_LR + LANES

ROW_TILE = 512
SEQ_TILE = 512
VMEM_LIMIT = 56 * 1024 * 1024

_F32 = jnp.float32
_BF16 = jnp.bfloat16
_NT = (((1,), (1,)), ((), ()))
_TN = (((0,), (0,)), ((), ()))


def _layer_norm(xf, g, b):
    mu = jnp.mean(xf, axis=-1, keepdims=True)
    xc = xf - mu
    var = jnp.mean(xc * xc, axis=-1, keepdims=True)
    return xc * lax.rsqrt(var + LN_EPS) * g + b


def _silu(x):
    return x * jax.nn.sigmoid(x)


def _dot(a, b):
    return jnp.dot(a, b, preferred_element_type=_F32)


def _proj_kernel(x_ref, g_ref, b_ref, w_ref, wg_ref, bg_ref,
                 aqkv_ref, ag_ref, gqk_ref, gv_ref, gg_ref, loga_ref):
    hb = _layer_norm(x_ref[...], g_ref[...], b_ref[...]).astype(_BF16)
    aq = _dot(hb, w_ref[:, 0:D_ATT]) * (ATT_HEAD_DIM ** -0.5)
    aqkv_ref[:, 0:D_ATT] = aq.astype(_BF16)
    aqkv_ref[:, D_ATT:3 * D_ATT] = _dot(hb, w_ref[:, D_ATT:3 * D_ATT]).astype(_BF16)
    ag_ref[...] = _dot(hb, w_ref[:, _OFF_AG:_OFF_GQK])
    gqk_ref[:, 0:D_GLA_K] = _dot(hb, w_ref[:, _OFF_GQK:_OFF_GQK + D_GLA_K]) * (GLA_HEAD_K ** -0.5)
    gqk_ref[:, D_GLA_K:] = _dot(hb, w_ref[:, _OFF_GQK + D_GLA_K:_OFF_GV])
    gv_ref[...] = _dot(hb, w_ref[:, _OFF_GV:_OFF_GG]).astype(_BF16)
    gg_ref[...] = _dot(hb, w_ref[:, _OFF_GG:_OFF_LR])
    code = _dot(hb, w_ref[:, _OFF_LR:D_IN_PAD]).astype(_BF16)
    logit = _dot(code, wg_ref[...]) + bg_ref[...]
    loga_ref[...] = jax.nn.log_sigmoid(logit) * (1.0 / GLA_TAU)


def _mixer_kernel(aqkv_ref, ag_ref, gqk_ref, gv_ref, gg_ref, loga_ref, bias_ref, gnorm_ref,
                  mix_ref, k_hist, v_hist, state_ref, *, seq_tile):
    s = pl.program_id(1)

    @pl.when(s == 0)
    def _():
        k_hist[0:LEFT, :] = jnp.zeros((LEFT, D_ATT), _BF16)
        v_hist[0:LEFT, :] = jnp.zeros((LEFT, D_ATT), _BF16)
        state_ref[...] = jnp.zeros_like(state_ref)

    @pl.when(s > 0)
    def _():
        k_hist[0:LEFT, :] = k_hist[seq_tile:seq_tile + LEFT, :]
        v_hist[0:LEFT, :] = v_hist[seq_tile:seq_tile + LEFT, :]

    k_hist[LEFT:LEFT + seq_tile, :] = aqkv_ref[:, D_ATT:2 * D_ATT]
    v_hist[LEFT:LEFT + seq_tile, :] = aqkv_ref[:, 2 * D_ATT:3 * D_ATT]

    lane = lax.broadcasted_iota(jnp.int32, (1, LANES), 1)
    low_half = lane < ATT_HEAD_DIM
    key_pos = lax.broadcasted_iota(jnp.int32, (1, BAND), 1)
    row_i = lax.broadcasted_iota(jnp.int32, (CHUNK, CHUNK), 0)
    col_i = lax.broadcasted_iota(jnp.int32, (CHUNK, CHUNK), 1)
    causal = row_i >= col_i
    tril_ones = causal.astype(_BF16)
    col4 = lax.broadcasted_iota(jnp.int32, (1, D_GLA_K), 1) // GLA_HEAD_K

    def chunk_body(c, carry):
        r0 = pl.multiple_of(c * CHUNK, CHUNK)
        rows = pl.ds(r0, CHUNK)
        band = pl.ds(r0, BAND)
        valid = key_pos >= LEFT - (s * seq_tile + c * CHUNK)

        for j in range(N_ATT_HEADS // 2):
            cols = slice(j * LANES, (j + 1) * LANES)
            qp = aqkv_ref[rows, cols]
            kp = k_hist[band, cols]
            vp = v_hist[band, cols]
            halves = []
            for e in range(2):
                sel = low_half if e == 0 else jnp.logical_not(low_half)
                qm = jnp.where(sel, qp, jnp.zeros_like(qp))
                sc = lax.dot_general(qm, kp, _NT, preferred_element_type=_F32)
                sc = sc + bias_ref[2 * j + e]
                sc = jnp.where(valid, sc, MASK_VALUE)
                m = jnp.max(sc, axis=-1, keepdims=True)
                pe = jnp.exp(sc - m)
                denom = jnp.sum(pe, axis=-1, keepdims=True)
                o = _dot(pe.astype(_BF16), vp)
                halves.append(o * (1.0 / denom))
            att = jnp.where(low_half, halves[0], halves[1])
            mix_ref[rows, cols] = (att * _silu(ag_ref[rows, cols])).astype(_BF16)

        la = loga_ref[rows, :]
        la_hi = la.astype(_BF16)
        rem = la - la_hi.astype(_F32)
        la_mid = rem.astype(_BF16)
        la_lo = (rem - la_mid.astype(_F32)).astype(_BF16)
        cum = _dot(tril_ones, la_hi) + _dot(tril_ones, la_mid) + _dot(tril_ones, la_lo)
        cum_end = cum[CHUNK - 1:CHUNK, :]
        e_pos = jnp.exp(cum)
        e_neg = jnp.exp(-cum)
        gq = gqk_ref[rows, 0:D_GLA_K]
        gk = gqk_ref[rows, D_GLA_K:2 * D_GLA_K]
        q_fwd = gq * e_pos
        q_bwd = gq * e_neg
        k_fwd = (gk * e_pos).astype(_BF16)
        k_bwd = (gk * e_neg).astype(_BF16)
        k_end = (gk * jnp.exp(cum_end - cum)).astype(_BF16)
        state = state_ref[...]
        state_b = state.astype(_BF16)
        new_state = jnp.zeros_like(state)
        for h in range(N_GLA_HEADS):
            pcols = slice((h // 2) * LANES, (h // 2 + 1) * LANES)
            vcols = slice(D_ATT + h * GLA_HEAD_V, D_ATT + (h + 1) * GLA_HEAD_V)
            sel = low_half if h % 2 == 0 else jnp.logical_not(low_half)
            qf = jnp.where(sel, q_fwd[:, pcols], 0.0).astype(_BF16)
            qb = jnp.where(sel, q_bwd[:, pcols], 0.0).astype(_BF16)
            a_causal = lax.dot_general(qf, k_bwd[:, pcols], _NT, preferred_element_type=_F32)
            a_anti = lax.dot_general(qb, k_fwd[:, pcols], _NT, preferred_element_type=_F32)
            att = jnp.where(causal, a_causal, a_anti).astype(_BF16)
            v_h = gv_ref[rows, h * GLA_HEAD_V:(h + 1) * GLA_HEAD_V]
            o = _dot(att, v_h)
            o = o + lax.dot_general(qf, state_b[:, pcols], _NT, preferred_element_type=_F32)
            o = o * lax.rsqrt(jnp.mean(o * o, axis=-1, keepdims=True) + RMS_EPS) * gnorm_ref[h:h + 1, :]
            mix_ref[rows, vcols] = (o * _silu(gg_ref[rows, h * GLA_HEAD_V:(h + 1) * GLA_HEAD_V])).astype(_BF16)
            kv_h = lax.dot_general(v_h, k_end, _TN, preferred_element_type=_F32)
            new_state = jnp.where(col4 == h, kv_h, new_state)
        state_ref[...] = jnp.exp(cum_end) * state + new_state
        return carry

    lax.fori_loop(0, seq_tile // CHUNK, chunk_body, 0)


def _epilogue_kernel(x_ref, g_ref, b_ref, mix_ref, p_ref, wo_ref, wpg_ref, bpg_ref, wp_ref,
                     og_ref, ob_ref, out_ref, *, alpha):
    h = _layer_norm(x_ref[...], g_ref[...], b_ref[...])
    r = alpha * h + _dot(mix_ref[...], wo_ref[...])
    gate = jax.nn.sigmoid(_dot(r.astype(_BF16), wpg_ref[...]) + bpg_ref[...])
    r = r + gate * _dot(p_ref[...].astype(_BF16), wp_ref[...])
    out_ref[...] = _layer_norm(r, og_ref[...], ob_ref[...])


def _const_spec(shape):
    zeros = (0,) * len(shape)
    return pl.BlockSpec(shape, lambda *_: zeros, pipeline_mode=pl.Buffered(1))


def _rel_bias_band(rel_table):
    qi = jnp.arange(CHUNK)[:, None]
    kj = jnp.arange(BAND)[None, :]
    idx = jnp.clip(qi + LEFT - kj, -REL_CLIP, REL_CLIP) + REL_CLIP
    return rel_table.astype(_F32)[:, idx]


def kernel(x, p, ln_in_g, ln_in_b, w_in, w_gla_gate, b_gla_gate, rel_bias,
           gla_norm_g, w_out, w_ple, w_ple_gate, b_ple_gate, ln_g, ln_b):
    batch, seq, d_model = x.shape
    depth = w_in.shape[0]
    assert depth == 1, "single-layer stack only"
    assert w_in.shape[2] == D_IN_PROJ
    assert seq % SEQ_TILE == 0 and SEQ_TILE % CHUNK == 0 and SEQ_TILE >= LEFT
    n = batch * seq
    assert n % ROW_TILE == 0
    d_ple = p.shape[-1]
    d_mix = D_ATT + D_GLA_V
    alpha = (2.0 * depth) ** 0.25

    x2 = x.reshape(n, d_model)
    p2 = p[0].reshape(n, d_ple)
    row = lambda v: v.reshape(1, -1).astype(_F32)
    w_in_b = jnp.pad(w_in[0], ((0, 0), (0, D_IN_PAD - D_IN_PROJ))).astype(_BF16)
    wg_b = jnp.pad(w_gla_gate[0], ((0, LANES - GLA_LOW_RANK), (0, 0))).astype(_BF16)
    bias = _rel_bias_band(rel_bias[0])

    cparams = functools.partial(pltpu.CompilerParams, vmem_limit_bytes=VMEM_LIMIT)
    row_spec = lambda width: pl.BlockSpec((ROW_TILE, width), lambda i: (i, 0))

    aqkv, ag, gqk, gv, gg, loga = pl.pallas_call(
        _proj_kernel,
        grid=(n // ROW_TILE,),
        in_specs=[row_spec(d_model), _const_spec((1, d_model)), _const_spec((1, d_model)),
                  _const_spec((d_model, D_IN_PAD)), _const_spec((LANES, D_GLA_K)),
                  _const_spec((1, D_GLA_K))],
        out_specs=[row_spec(3 * D_ATT), row_spec(D_ATT), row_spec(2 * D_GLA_K),
                   row_spec(D_GLA_V), row_spec(D_GLA_V), row_spec(D_GLA_K)],
        out_shape=[jax.ShapeDtypeStruct((n, 3 * D_ATT), _BF16),
                   jax.ShapeDtypeStruct((n, D_ATT), _F32),
                   jax.ShapeDtypeStruct((n, 2 * D_GLA_K), _F32),
                   jax.ShapeDtypeStruct((n, D_GLA_V), _BF16),
                   jax.ShapeDtypeStruct((n, D_GLA_V), _F32),
                   jax.ShapeDtypeStruct((n, D_GLA_K), _F32)],
        compiler_params=cparams(dimension_semantics=("parallel",)),
        name="ln_in_proj",
    )(x2, row(ln_in_g), row(ln_in_b), w_in_b, wg_b, row(b_gla_gate[0]))

    tiles = seq // SEQ_TILE
    seq_spec = lambda width: pl.BlockSpec((SEQ_TILE, width), lambda b, s: (b * tiles + s, 0))
    mix = pl.pallas_call(
        functools.partial(_mixer_kernel, seq_tile=SEQ_TILE),
        grid=(batch, tiles),
        in_specs=[seq_spec(3 * D_ATT), seq_spec(D_ATT), seq_spec(2 * D_GLA_K), seq_spec(D_GLA_V),
                  seq_spec(D_GLA_V), seq_spec(D_GLA_K),
                  _const_spec((N_ATT_HEADS, CHUNK, BAND)), _const_spec((N_GLA_HEADS, GLA_HEAD_V))],
        out_specs=seq_spec(d_mix),
        out_shape=jax.ShapeDtypeStruct((n, d_mix), _BF16),
        scratch_shapes=[pltpu.VMEM((LEFT + SEQ_TILE, D_ATT), _BF16),
                        pltpu.VMEM((LEFT + SEQ_TILE, D_ATT), _BF16),
                        pltpu.VMEM((GLA_HEAD_V, D_GLA_K), _F32)],
        compiler_params=cparams(dimension_semantics=("arbitrary", "arbitrary")),
        name="mixers",
    )(aqkv, ag, gqk, gv, gg, loga, bias, gla_norm_g[0].astype(_F32))

    out = pl.pallas_call(
        functools.partial(_epilogue_kernel, alpha=alpha),
        grid=(n // ROW_TILE,),
        in_specs=[row_spec(d_model), _const_spec((1, d_model)), _const_spec((1, d_model)),
                  row_spec(d_mix), row_spec(d_ple),
                  _const_spec((d_mix, d_model)), _const_spec((d_model, d_model)),
                  _const_spec((1, d_model)), _const_spec((d_ple, d_model)),
                  _const_spec((1, d_model)), _const_spec((1, d_model))],
        out_specs=row_spec(d_model),
        out_shape=jax.ShapeDtypeStruct((n, d_model), x.dtype),
        compiler_params=cparams(dimension_semantics=("parallel",)),
        name="out_proj_ple_norm",
    )(x2, row(ln_in_g), row(ln_in_b), mix, p2, w_out[0].astype(_BF16), w_ple_gate[0].astype(_BF16),
      row(b_ple_gate[0]), w_ple[0].astype(_BF16), row(ln_g[0]), row(ln_b[0]))
    return out.reshape(batch, seq, d_model)
```

```python
import functools

import jax
import jax.numpy as jnp
from jax import lax
from jax.experimental import pallas as pl
from jax.experimental.pallas import tpu as pltpu

CHUNK = 64
N_ATT_HEADS = 8
ATT_HEAD_DIM = 64
D_ATT = N_ATT_HEADS * ATT_HEAD_DIM
LEFT_CHUNKS = 8
LEFT = LEFT_CHUNKS * CHUNK
BAND = LEFT + CHUNK
REL_CLIP = 128
N_GLA_HEADS = 4
GLA_HEAD_K = 64
GLA_HEAD_V = 128
D_GLA_K = N_GLA_HEADS * GLA_HEAD_K
D_GLA_V = N_GLA_HEADS * GLA_HEAD_V
GLA_LOW_RANK = 16
GLA_TAU = 16.0
LN_EPS = 1e-5
RMS_EPS = 1e-6
MASK_VALUE = -1e30

LANES = 128
HEAD_PAIR = 2 * ATT_HEAD_DIM
assert HEAD_PAIR == LANES and 2 * GLA_HEAD_K == LANES

_OFF_AQKV = 0
_OFF_AG = 3 * D_ATT
_OFF_GQK = 4 * D_ATT
_OFF_GV = _OFF_GQK + 2 * D_GLA_K
_OFF_GG = _OFF_GV + D_GLA_V
_OFF_LR = _OFF_GG + D_GLA_V
D_IN_PROJ = _OFF_LR + GLA_LOW_RANK
D_IN_PAD = _OFF_LR + LANES

ROW_TILE = 512
SEQ_TILE = 512
VMEM_LIMIT = 56 * 1024 * 1024

_F32 = jnp.float32
_BF16 = jnp.bfloat16
_NT = (((1,), (1,)), ((), ()))
_TN = (((0,), (0,)), ((), ()))


def _layer_norm(xf, g, b):
    mu = jnp.mean(xf, axis=-1, keepdims=True)
    xc = xf - mu
    var = jnp.mean(xc * xc, axis=-1, keepdims=True)
    return xc * lax.rsqrt(var + LN_EPS) * g + b


def _silu(x):
    return x * jax.nn.sigmoid(x)


def _dot(a, b):
    return jnp.dot(a, b, preferred_element_type=_F32)


def _proj_kernel(x_ref, g_ref, b_ref, w_ref, wg_ref, bg_ref,
                 aqkv_ref, ag_ref, gqk_ref, gv_ref, gg_ref, loga_ref):
    hb = _layer_norm(x_ref[...], g_ref[...], b_ref[...]).astype(_BF16)
    aq = _dot(hb, w_ref[:, 0:D_ATT]) * (ATT_HEAD_DIM ** -0.5)
    aqkv_ref[:, 0:D_ATT] = aq.astype(_BF16)
    aqkv_ref[:, D_ATT:3 * D_ATT] = _dot(hb, w_ref[:, D_ATT:3 * D_ATT]).astype(_BF16)
    ag_ref[...] = _dot(hb, w_ref[:, _OFF_AG:_OFF_GQK])
    gqk_ref[:, 0:D_GLA_K] = _dot(hb, w_ref[:, _OFF_GQK:_OFF_GQK + D_GLA_K]) * (GLA_HEAD_K ** -0.5)
    gqk_ref[:, D_GLA_K:] = _dot(hb, w_ref[:, _OFF_GQK + D_GLA_K:_OFF_GV])
    gv_ref[...] = _dot(hb, w_ref[:, _OFF_GV:_OFF_GG]).astype(_BF16)
    gg_ref[...] = _dot(hb, w_ref[:, _OFF_GG:_OFF_LR])
    code = _dot(hb, w_ref[:, _OFF_LR:D_IN_PAD]).astype(_BF16)
    logit = _dot(code, wg_ref[...]) + bg_ref[...]
    loga_ref[...] = jax.nn.log_sigmoid(logit) * (1.0 / GLA_TAU)


def _mixer_kernel(aqkv_ref, ag_ref, gqk_ref, gv_ref, gg_ref, loga_ref, bias_ref, gnorm_ref,
                  mix_ref, k_hist, v_hist, state_ref, *, seq_tile):
    s = pl.program_id(1)

    @pl.when(s == 0)
    def _():
        k_hist[0:LEFT, :] = jnp.zeros((LEFT, D_ATT), _BF16)
        v_hist[0:LEFT, :] = jnp.zeros((LEFT, D_ATT), _BF16)
        state_ref[...] = jnp.zeros_like(state_ref)

    @pl.when(s > 0)
    def _():
        k_hist[0:LEFT, :] = k_hist[seq_tile:seq_tile + LEFT, :]
        v_hist[0:LEFT, :] = v_hist[seq_tile:seq_tile + LEFT, :]

    k_hist[LEFT:LEFT + seq_tile, :] = aqkv_ref[:, D_ATT:2 * D_ATT]
    v_hist[LEFT:LEFT + seq_tile, :] = aqkv_ref[:, 2 * D_ATT:3 * D_ATT]

    lane = lax.broadcasted_iota(jnp.int32, (1, LANES), 1)
    low_half = lane < ATT_HEAD_DIM
    key_pos = lax.broadcasted_iota(jnp.int32, (1, BAND), 1)
    row_i = lax.broadcasted_iota(jnp.int32, (CHUNK, CHUNK), 0)
    col_i = lax.broadcasted_iota(jnp.int32, (CHUNK, CHUNK), 1)
    causal = row_i >= col_i
    tril_ones = causal.astype(_BF16)
    col4 = lax.broadcasted_iota(jnp.int32, (1, D_GLA_K), 1) // GLA_HEAD_K

    def chunk_body(c, carry):
        r0 = pl.multiple_of(c * CHUNK, CHUNK)
        rows = pl.ds(r0, CHUNK)
        band = pl.ds(r0, BAND)
        valid = key_pos >= LEFT - (s * seq_tile + c * CHUNK)

        def qk(hh):
            j, e = divmod(hh, 2)
            cols = slice(j * LANES, (j + 1) * LANES)
            qp = aqkv_ref[rows, cols]
            kp = k_hist[band, cols]
            sel = low_half if e == 0 else jnp.logical_not(low_half)
            qm = jnp.where(sel, qp, jnp.zeros_like(qp))
            sc = lax.dot_general(qm, kp, _NT, preferred_element_type=_F32)
            sc = sc + bias_ref[hh]
            return jnp.where(valid, sc, MASK_VALUE)

        def sm_pv(hh, sc):
            j, e = divmod(hh, 2)
            cols = slice(j * LANES, (j + 1) * LANES)
            vp = v_hist[band, cols]
            m = jnp.max(sc, axis=-1, keepdims=True)
            pe = jnp.exp(sc - m)
            denom = jnp.sum(pe, axis=-1, keepdims=True)
            o = _dot(pe.astype(_BF16), vp)
            return o * (1.0 / denom)

        LOOK = 2
        scs = {}
        outs = {}
        for step in range(N_ATT_HEADS + LOOK):
            if step < N_ATT_HEADS:
                scs[step] = qk(step)
            if step >= LOOK:
                hh = step - LOOK
                outs[hh] = sm_pv(hh, scs.pop(hh))
                if hh % 2 == 1:
                    j = hh // 2
                    cols = slice(j * LANES, (j + 1) * LANES)
                    att = jnp.where(low_half, outs.pop(hh - 1), outs.pop(hh))
                    mix_ref[rows, cols] = (att * _silu(ag_ref[rows, cols])).astype(_BF16)

        la = loga_ref[rows, :]
        la_hi = la.astype(_BF16)
        rem = la - la_hi.astype(_F32)
        la_mid = rem.astype(_BF16)
        la_lo = (rem - la_mid.astype(_F32)).astype(_BF16)
        cum = _dot(tril_ones, la_hi) + _dot(tril_ones, la_mid) + _dot(tril_ones, la_lo)
        cum_end = cum[CHUNK - 1:CHUNK, :]
        e_pos = jnp.exp(cum)
        e_neg = jnp.exp(-cum)
        gq = gqk_ref[rows, 0:D_GLA_K]
        gk = gqk_ref[rows, D_GLA_K:2 * D_GLA_K]
        q_fwd = gq * e_pos
        q_bwd = gq * e_neg
        k_fwd = (gk * e_pos).astype(_BF16)
        k_bwd = (gk * e_neg).astype(_BF16)
        k_end = (gk * jnp.exp(cum_end - cum)).astype(_BF16)
        state = state_ref[...]
        state_b = state.astype(_BF16)
        new_state = jnp.zeros_like(state)
        for h in range(N_GLA_HEADS):
            pcols = slice((h // 2) * LANES, (h // 2 + 1) * LANES)
            vcols = slice(D_ATT + h * GLA_HEAD_V, D_ATT + (h + 1) * GLA_HEAD_V)
            sel = low_half if h % 2 == 0 else jnp.logical_not(low_half)
            qf = jnp.where(sel, q_fwd[:, pcols], 0.0).astype(_BF16)
            qb = jnp.where(sel, q_bwd[:, pcols], 0.0).astype(_BF16)
            a_causal = lax.dot_general(qf, k_bwd[:, pcols], _NT, preferred_element_type=_F32)
            a_anti = lax.dot_general(qb, k_fwd[:, pcols], _NT, preferred_element_type=_F32)
            att = jnp.where(causal, a_causal, a_anti).astype(_BF16)
            v_h = gv_ref[rows, h * GLA_HEAD_V:(h + 1) * GLA_HEAD_V]
            o = _dot(att, v_h)
            o = o + lax.dot_general(qf, state_b[:, pcols], _NT, preferred_element_type=_F32)
            o = o * lax.rsqrt(jnp.mean(o * o, axis=-1, keepdims=True) + RMS_EPS) * gnorm_ref[h:h + 1, :]
            mix_ref[rows, vcols] = (o * _silu(gg_ref[rows, h * GLA_HEAD_V:(h + 1) * GLA_HEAD_V])).astype(_BF16)
            kv_h = lax.dot_general(v_h, k_end, _TN, preferred_element_type=_F32)
            new_state = jnp.where(col4 == h, kv_h, new_state)
        state_ref[...] = jnp.exp(cum_end) * state + new_state
        return carry

    lax.fori_loop(0, seq_tile // CHUNK, chunk_body, 0)


def _epilogue_kernel(x_ref, g_ref, b_ref, mix_ref, p_ref, wo_ref, wpg_ref, bpg_ref, wp_ref,
                     og_ref, ob_ref, out_ref, *, alpha):
    h = _layer_norm(x_ref[...], g_ref[...], b_ref[...])
    r = alpha * h + _dot(mix_ref[...], wo_ref[...])
    gate = jax.nn.sigmoid(_dot(r.astype(_BF16), wpg_ref[...]) + bpg_ref[...])
    r = r + gate * _dot(p_ref[...].astype(_BF16), wp_ref[...])
    out_ref[...] = _layer_norm(r, og_ref[...], ob_ref[...])


def _const_spec(shape):
    zeros = (0,) * len(shape)
    return pl.BlockSpec(shape, lambda *_: zeros, pipeline_mode=pl.Buffered(1))


def _rel_bias_band(rel_table):
    span = CHUNK - 1 + BAND
    ext_idx = jnp.clip(BAND - 1 - jnp.arange(span), -REL_CLIP, REL_CLIP) + REL_CLIP
    ext = rel_table.astype(_F32)[:, ext_idx]
    return jnp.stack([ext[:, CHUNK - 1 - q:CHUNK - 1 - q + BAND] for q in range(CHUNK)], axis=1)


def kernel(x, p, ln_in_g, ln_in_b, w_in, w_gla_gate, b_gla_gate, rel_bias,
           gla_norm_g, w_out, w_ple, w_ple_gate, b_ple_gate, ln_g, ln_b):
    batch, seq, d_model = x.shape
    depth = w_in.shape[0]
    assert depth == 1, "single-layer stack only"
    assert w_in.shape[2] == D_IN_PROJ
    assert seq % SEQ_TILE == 0 and SEQ_TILE % CHUNK == 0 and SEQ_TILE >= LEFT
    n = batch * seq
    assert n % ROW_TILE == 0
    d_ple = p.shape[-1]
    d_mix = D_ATT + D_GLA_V
    alpha = (2.0 * depth) ** 0.25

    x2 = x.reshape(n, d_model)
    p2 = p[0].reshape(n, d_ple)
    row = lambda v: v.reshape(1, -1).astype(_F32)
    w_in_b = jnp.pad(w_in[0], ((0, 0), (0, D_IN_PAD - D_IN_PROJ))).astype(_BF16)
    wg_b = jnp.pad(w_gla_gate[0], ((0, LANES - GLA_LOW_RANK), (0, 0))).astype(_BF16)
    bias = _rel_bias_band(rel_bias[0])

    cparams = functools.partial(pltpu.CompilerParams, vmem_limit_bytes=VMEM_LIMIT)
    row_spec = lambda width: pl.BlockSpec((ROW_TILE, width), lambda i: (i, 0))

    aqkv, ag, gqk, gv, gg, loga = pl.pallas_call(
        _proj_kernel,
        grid=(n // ROW_TILE,),
        in_specs=[row_spec(d_model), _const_spec((1, d_model)), _const_spec((1, d_model)),
                  _const_spec((d_model, D_IN_PAD)), _const_spec((LANES, D_GLA_K)),
                  _const_spec((1, D_GLA_K))],
        out_specs=[row_spec(3 * D_ATT), row_spec(D_ATT), row_spec(2 * D_GLA_K),
                   row_spec(D_GLA_V), row_spec(D_GLA_V), row_spec(D_GLA_K)],
        out_shape=[jax.ShapeDtypeStruct((n, 3 * D_ATT), _BF16),
                   jax.ShapeDtypeStruct((n, D_ATT), _F32),
                   jax.ShapeDtypeStruct((n, 2 * D_GLA_K), _F32),
                   jax.ShapeDtypeStruct((n, D_GLA_V), _BF16),
                   jax.ShapeDtypeStruct((n, D_GLA_V), _F32),
                   jax.ShapeDtypeStruct((n, D_GLA_K), _F32)],
        compiler_params=cparams(dimension_semantics=("parallel",)),
        name="ln_in_proj",
    )(x2, row(ln_in_g), row(ln_in_b), w_in_b, wg_b, row(b_gla_gate[0]))

    tiles = seq // SEQ_TILE
    seq_spec = lambda width: pl.BlockSpec((SEQ_TILE, width), lambda b, s: (b * tiles + s, 0))
    mix = pl.pallas_call(
        functools.partial(_mixer_kernel, seq_tile=SEQ_TILE),
        grid=(batch, tiles),
        in_specs=[seq_spec(3 * D_ATT), seq_spec(D_ATT), seq_spec(2 * D_GLA_K), seq_spec(D_GLA_V),
                  seq_spec(D_GLA_V), seq_spec(D_GLA_K),
                  _const_spec((N_ATT_HEADS, CHUNK, BAND)), _const_spec((N_GLA_HEADS, GLA_HEAD_V))],
        out_specs=seq_spec(d_mix),
        out_shape=jax.ShapeDtypeStruct((n, d_mix), _BF16),
        scratch_shapes=[pltpu.VMEM((LEFT + SEQ_TILE, D_ATT), _BF16),
                        pltpu.VMEM((LEFT + SEQ_TILE, D_ATT), _BF16),
                        pltpu.VMEM((GLA_HEAD_V, D_GLA_K), _F32)],
        compiler_params=cparams(dimension_semantics=("arbitrary", "arbitrary")),
        name="mixers",
    )(aqkv, ag, gqk, gv, gg, loga, bias, gla_norm_g[0].astype(_F32))

    out = pl.pallas_call(
        functools.partial(_epilogue_kernel, alpha=alpha),
        grid=(n // ROW_TILE,),
        in_specs=[row_spec(d_model), _const_spec((1, d_model)), _const_spec((1, d_model)),
                  row_spec(d_mix), row_spec(d_ple),
                  _const_spec((d_mix, d_model)), _const_spec((d_model, d_model)),
                  _const_spec((1, d_model)), _const_spec((d_ple, d_model)),
                  _const_spec((1, d_model)), _const_spec((1, d_model))],
        out_specs=row_spec(d_model),
        out_shape=jax.ShapeDtypeStruct((n, d_model), x.dtype),
        compiler_params=cparams(dimension_semantics=("parallel",)),
        name="out_proj_ple_norm",
    )(x2, row(ln_in_g), row(ln_in_b), mix, p2, w_out[0].astype(_BF16), w_ple_gate[0].astype(_BF16),
      row(b_ple_gate[0]), w_ple[0].astype(_BF16), row(ln_g[0]), row(ln_b[0]))
    return out.reshape(batch, seq, d_model)
```

```python
import functools
import math

import jax
import jax.numpy as jnp
from jax import lax
from jax.experimental import pallas as pl
from jax.experimental.pallas import tpu as pltpu

CHUNK = 64
N_ATT_HEADS = 8
ATT_HEAD_DIM = 64
D_ATT = N_ATT_HEADS * ATT_HEAD_DIM
LEFT_CHUNKS = 8
LEFT = LEFT_CHUNKS * CHUNK
BAND = LEFT + CHUNK
REL_CLIP = 128
N_GLA_HEADS = 4
GLA_HEAD_K = 64
GLA_HEAD_V = 128
D_GLA_K = N_GLA_HEADS * GLA_HEAD_K
D_GLA_V = N_GLA_HEADS * GLA_HEAD_V
GLA_LOW_RANK = 16
GLA_TAU = 16.0
LN_EPS = 1e-5
RMS_EPS = 1e-6
MASK_VALUE = -1e30
LOG2_E = math.log2(math.e)

LANES = 128
HEAD_PAIR = 2 * ATT_HEAD_DIM
assert HEAD_PAIR == LANES and 2 * GLA_HEAD_K == LANES
N_HEAD_PAIRS = N_ATT_HEADS // 2
PAIR_ROWS = 2 * CHUNK
KEY_SPAN = LEFT + PAIR_ROWS
KEY_BLOCK = LANES
N_SPAN_BLOCKS = KEY_SPAN // KEY_BLOCK
N_LEFT_BLOCKS = LEFT // KEY_BLOCK
assert PAIR_ROWS == KEY_BLOCK

_OFF_AG = 3 * D_ATT
_OFF_GQK = 4 * D_ATT
_OFF_GV = _OFF_GQK + 2 * D_GLA_K
_OFF_GG = _OFF_GV + D_GLA_V
_OFF_LR = _OFF_GG + D_GLA_V
D_IN_PROJ = _OFF_LR + GLA_LOW_RANK
D_IN_PAD = _OFF_LR + LANES

ROW_TILE = 512
SEQ_TILE = 512
VMEM_LIMIT = 56 * 1024 * 1024

_F32 = jnp.float32
_BF16 = jnp.bfloat16
_NT = (((1,), (1,)), ((), ()))
_TN = (((0,), (0,)), ((), ()))


def _layer_norm(xf, g, b):
    mu = jnp.mean(xf, axis=-1, keepdims=True)
    xc = xf - mu
    var = jnp.mean(xc * xc, axis=-1, keepdims=True)
    return xc * lax.rsqrt(var + LN_EPS) * g + b


def _silu(x):
    return x * jax.nn.sigmoid(x)


def _dot(a, b):
    return jnp.dot(a, b, preferred_element_type=_F32)


def _proj_kernel(x_ref, g_ref, b_ref, w_ref, wg_ref, bg_ref,
                 aqkv_ref, ag_ref, gqk_ref, gv_ref, gg_ref, loga_ref):
    hb = _layer_norm(x_ref[...], g_ref[...], b_ref[...]).astype(_BF16)
    aq = _dot(hb, w_ref[:, 0:D_ATT]) * (ATT_HEAD_DIM ** -0.5 * LOG2_E)
    aqkv_ref[:, 0:D_ATT] = aq.astype(_BF16)
    aqkv_ref[:, D_ATT:3 * D_ATT] = _dot(hb, w_ref[:, D_ATT:3 * D_ATT]).astype(_BF16)
    ag_ref[...] = _dot(hb, w_ref[:, _OFF_AG:_OFF_GQK])
    gqk_ref[:, 0:D_GLA_K] = _dot(hb, w_ref[:, _OFF_GQK:_OFF_GQK + D_GLA_K]) * (GLA_HEAD_K ** -0.5)
    gqk_ref[:, D_GLA_K:] = _dot(hb, w_ref[:, _OFF_GQK + D_GLA_K:_OFF_GV])
    gv_ref[...] = _dot(hb, w_ref[:, _OFF_GV:_OFF_GG]).astype(_BF16)
    gg_ref[...] = _dot(hb, w_ref[:, _OFF_GG:_OFF_LR])
    code = _dot(hb, w_ref[:, _OFF_LR:D_IN_PAD]).astype(_BF16)
    logit = _dot(code, wg_ref[...]) + bg_ref[...]
    loga_ref[...] = jax.nn.log_sigmoid(logit) * (1.0 / GLA_TAU)


def _mixer_kernel(aqkv_ref, ag_ref, gqk_ref, gv_ref, gg_ref, loga_ref, bias_ref, gnorm_ref,
                  mix_ref, k_hist, vt_hist, state_ref, *, seq_tile):
    s = pl.program_id(1)
    n_new_blocks = seq_tile // KEY_BLOCK

    @pl.when(s == 0)
    def _():
        k_hist[0:LEFT, :] = jnp.zeros((LEFT, D_ATT), _BF16)
        vt_hist[0:N_LEFT_BLOCKS] = jnp.zeros((N_LEFT_BLOCKS, D_ATT, KEY_BLOCK), _BF16)
        state_ref[...] = jnp.zeros_like(state_ref)

    @pl.when(s > 0)
    def _():
        k_hist[0:LEFT, :] = k_hist[seq_tile:seq_tile + LEFT, :]
        vt_hist[0:N_LEFT_BLOCKS] = vt_hist[n_new_blocks:n_new_blocks + N_LEFT_BLOCKS]

    k_hist[LEFT:LEFT + seq_tile, :] = aqkv_ref[:, D_ATT:2 * D_ATT]
    for blk in range(n_new_blocks):
        v_blk = aqkv_ref[blk * KEY_BLOCK:(blk + 1) * KEY_BLOCK, 2 * D_ATT:3 * D_ATT]
        vt_hist[N_LEFT_BLOCKS + blk] = v_blk.T

    lane = lax.broadcasted_iota(jnp.int32, (1, LANES), 1)
    low_half = lane < ATT_HEAD_DIM
    top_rows = lax.broadcasted_iota(jnp.int32, (LANES, 1), 0) < ATT_HEAD_DIM
    row_i = lax.broadcasted_iota(jnp.int32, (CHUNK, CHUNK), 0)
    col_i = lax.broadcasted_iota(jnp.int32, (CHUNK, CHUNK), 1)
    causal = row_i >= col_i
    tril_ones = causal.astype(_BF16)
    col4 = lax.broadcasted_iota(jnp.int32, (1, D_GLA_K), 1) // GLA_HEAD_K

    def pair_body(cp, carry):
        r0 = pl.multiple_of(cp * PAIR_ROWS, PAIR_ROWS)
        rows = pl.ds(r0, PAIR_ROWS)
        span = pl.ds(r0, KEY_SPAN)
        first_key = s * seq_tile + cp * PAIR_ROWS - LEFT
        block_valid = [first_key + i * KEY_BLOCK >= 0 for i in range(N_SPAN_BLOCKS)]

        def att_scores(j):
            cols = slice(j * LANES, (j + 1) * LANES)
            qp = aqkv_ref[rows, cols]
            zero = jnp.zeros_like(qp)
            q_blk = jnp.concatenate([jnp.where(low_half, qp, zero), jnp.where(low_half, zero, qp)], axis=0)
            kp = k_hist[span, cols]
            sc = lax.dot_general(kp, q_blk, _NT, preferred_element_type=_F32)
            sc = sc + bias_ref[j]
            blocks = [jnp.where(block_valid[i], sc[i * KEY_BLOCK:(i + 1) * KEY_BLOCK, :], MASK_VALUE)
                      for i in range(N_SPAN_BLOCKS)]
            return jnp.concatenate(blocks, axis=0)

        def att_softmax(sc):
            m = jnp.max(sc, axis=0, keepdims=True)
            pe = jnp.exp2(sc - m)
            denom = jnp.sum(pe, axis=0, keepdims=True)
            return pe.astype(_BF16), 1.0 / denom

        def att_out(j, pe, inv):
            cols = slice(j * LANES, (j + 1) * LANES)
            vt = jnp.concatenate([vt_hist[cp + i, cols, :] for i in range(N_SPAN_BLOCKS)], axis=1)
            ot = _dot(vt, pe)
            num = jnp.where(top_rows, ot[:, 0:LANES], ot[:, LANES:2 * LANES])
            scale = jnp.where(top_rows, inv[:, 0:LANES], inv[:, LANES:2 * LANES])
            att = (num * scale).T
            mix_ref[rows, cols] = (att * _silu(ag_ref[rows, cols])).astype(_BF16)

        def gla_cumsum(c):
            crow = pl.ds(pl.multiple_of(cp * PAIR_ROWS + c * CHUNK, CHUNK), CHUNK)
            la = loga_ref[crow, :]
            la_hi = la.astype(_BF16)
            rem = la - la_hi.astype(_F32)
            la_mid = rem.astype(_BF16)
            la_lo = (rem - la_mid.astype(_F32)).astype(_BF16)
            return _dot(tril_ones, la_hi) + _dot(tril_ones, la_mid) + _dot(tril_ones, la_lo)

        def gla_scores(c, cum):
            crow = pl.ds(pl.multiple_of(cp * PAIR_ROWS + c * CHUNK, CHUNK), CHUNK)
            cum_end = cum[CHUNK - 1:CHUNK, :]
            e_pos = jnp.exp(cum)
            e_neg = jnp.exp(-cum)
            gq = gqk_ref[crow, 0:D_GLA_K]
            gk = gqk_ref[crow, D_GLA_K:2 * D_GLA_K]
            q_fwd = gq * e_pos
            q_bwd = gq * e_neg
            k_fwd = (gk * e_pos).astype(_BF16)
            k_bwd = (gk * e_neg).astype(_BF16)
            k_end = (gk * jnp.exp(cum_end - cum)).astype(_BF16)
            state = state_ref[...]
            state_b = state.astype(_BF16)
            new_state = jnp.zeros_like(state)
            parts = []
            for h in range(N_GLA_HEADS):
                pcols = slice((h // 2) * LANES, (h // 2 + 1) * LANES)
                sel = low_half if h % 2 == 0 else jnp.logical_not(low_half)
                qf = jnp.where(sel, q_fwd[:, pcols], 0.0).astype(_BF16)
                qb = jnp.where(sel, q_bwd[:, pcols], 0.0).astype(_BF16)
                a_causal = lax.dot_general(qf, k_bwd[:, pcols], _NT, preferred_element_type=_F32)
                a_anti = lax.dot_general(qb, k_fwd[:, pcols], _NT, preferred_element_type=_F32)
                o_inter = lax.dot_general(qf, state_b[:, pcols], _NT, preferred_element_type=_F32)
                v_h = gv_ref[crow, h * GLA_HEAD_V:(h + 1) * GLA_HEAD_V]
                kv_h = lax.dot_general(v_h, k_end, _TN, preferred_element_type=_F32)
                new_state = jnp.where(col4 == h, kv_h, new_state)
                parts.append((a_causal, a_anti, o_inter))
            state_ref[...] = jnp.exp(cum_end) * state + new_state
            return parts

        def gla_out(c, parts):
            crow = pl.ds(pl.multiple_of(cp * PAIR_ROWS + c * CHUNK, CHUNK), CHUNK)
            for h in range(N_GLA_HEADS):
                a_causal, a_anti, o_inter = parts[h]
                hv = slice(h * GLA_HEAD_V, (h + 1) * GLA_HEAD_V)
                att = jnp.where(causal, a_causal, a_anti).astype(_BF16)
                o = _dot(att, gv_ref[crow, hv]) + o_inter
                o = o * lax.rsqrt(jnp.mean(o * o, axis=-1, keepdims=True) + RMS_EPS) * gnorm_ref[h:h + 1, :]
                mix_ref[crow, D_ATT + h * GLA_HEAD_V:D_ATT + (h + 1) * GLA_HEAD_V] = (
                    o * _silu(gg_ref[crow, hv])).astype(_BF16)

        cum0 = gla_cumsum(0)
        cum1 = gla_cumsum(1)
        sc0 = att_scores(0)
        sc1 = att_scores(1)
        g0 = gla_scores(0, cum0)
        p0 = att_softmax(sc0)
        att_out(0, *p0)
        sc2 = att_scores(2)
        gla_out(0, g0)
        g1 = gla_scores(1, cum1)
        p1 = att_softmax(sc1)
        att_out(1, *p1)
        sc3 = att_scores(3)
        gla_out(1, g1)
        p2 = att_softmax(sc2)
        att_out(2, *p2)
        p3 = att_softmax(sc3)
        att_out(3, *p3)
        return carry

    lax.fori_loop(0, seq_tile // PAIR_ROWS, pair_body, 0)


def _epilogue_kernel(x_ref, g_ref, b_ref, mix_ref, p_ref, wo_ref, wpg_ref, bpg_ref, wp_ref,
                     og_ref, ob_ref, out_ref, *, alpha):
    h = _layer_norm(x_ref[...], g_ref[...], b_ref[...])
    r = alpha * h + _dot(mix_ref[...], wo_ref[...])
    gate = jax.nn.sigmoid(_dot(r.astype(_BF16), wpg_ref[...]) + bpg_ref[...])
    r = r + gate * _dot(p_ref[...].astype(_BF16), wp_ref[...])
    out_ref[...] = _layer_norm(r, og_ref[...], ob_ref[...])


def _const_spec(shape):
    zeros = (0,) * len(shape)
    return pl.BlockSpec(shape, lambda *_: zeros, pipeline_mode=pl.Buffered(1))


def _rel_bias_span(rel_table):
    table = rel_table.astype(_F32)
    width = CHUNK - 1 + BAND
    ext_idx = jnp.clip(BAND - 1 - jnp.arange(width), -REL_CLIP, REL_CLIP) + REL_CLIP
    ext = table[:, ext_idx] - table[:, 2 * REL_CLIP:2 * REL_CLIP + 1]
    band = jnp.stack([ext[:, CHUNK - 1 - q:CHUNK - 1 - q + BAND] for q in range(CHUNK)], axis=1) * LOG2_E
    pad = lambda lo, hi: jnp.pad(band, ((0, 0), (0, 0), (lo, hi)), constant_values=MASK_VALUE)
    both = jnp.stack([pad(0, CHUNK), pad(CHUNK, 0)], axis=1)
    both = both.reshape(N_HEAD_PAIRS, 2, 2, CHUNK, KEY_SPAN)
    return both.transpose(0, 4, 1, 2, 3).reshape(N_HEAD_PAIRS, KEY_SPAN, 2 * PAIR_ROWS)


def kernel(x, p, ln_in_g, ln_in_b, w_in, w_gla_gate, b_gla_gate, rel_bias,
           gla_norm_g, w_out, w_ple, w_ple_gate, b_ple_gate, ln_g, ln_b):
    batch, seq, d_model = x.shape
    depth = w_in.shape[0]
    assert depth == 1, "single-layer stack only"
    assert w_in.shape[2] == D_IN_PROJ
    assert seq % SEQ_TILE == 0 and SEQ_TILE % PAIR_ROWS == 0 and SEQ_TILE >= LEFT
    n = batch * seq
    assert n % ROW_TILE == 0
    d_ple = p.shape[-1]
    d_mix = D_ATT + D_GLA_V
    alpha = (2.0 * depth) ** 0.25

    x2 = x.reshape(n, d_model)
    p2 = p[0].reshape(n, d_ple)
    row = lambda v: v.reshape(1, -1).astype(_F32)
    w_in_b = jnp.pad(w_in[0], ((0, 0), (0, D_IN_PAD - D_IN_PROJ))).astype(_BF16)
    wg_b = jnp.pad(w_gla_gate[0], ((0, LANES - GLA_LOW_RANK), (0, 0))).astype(_BF16)
    bias = _rel_bias_span(rel_bias[0])

    cparams = functools.partial(pltpu.CompilerParams, vmem_limit_bytes=VMEM_LIMIT)
    row_spec = lambda width: pl.BlockSpec((ROW_TILE, width), lambda i: (i, 0))

    aqkv, ag, gqk, gv, gg, loga = pl.pallas_call(
        _proj_kernel,
        grid=(n // ROW_TILE,),
        in_specs=[row_spec(d_model), _const_spec((1, d_model)), _const_spec((1, d_model)),
                  _const_spec((d_model, D_IN_PAD)), _const_spec((LANES, D_GLA_K)),
                  _const_spec((1, D_GLA_K))],
        out_specs=[row_spec(3 * D_ATT), row_spec(D_ATT), row_spec(2 * D_GLA_K),
                   row_spec(D_GLA_V), row_spec(D_GLA_V), row_spec(D_GLA_K)],
        out_shape=[jax.ShapeDtypeStruct((n, 3 * D_ATT), _BF16),
                   jax.ShapeDtypeStruct((n, D_ATT), _F32),
                   jax.ShapeDtypeStruct((n, 2 * D_GLA_K), _F32),
                   jax.ShapeDtypeStruct((n, D_GLA_V), _BF16),
                   jax.ShapeDtypeStruct((n, D_GLA_V), _F32),
                   jax.ShapeDtypeStruct((n, D_GLA_K), _F32)],
        compiler_params=cparams(dimension_semantics=("parallel",)),
        name="ln_in_proj",
    )(x2, row(ln_in_g), row(ln_in_b), w_in_b, wg_b, row(b_gla_gate[0]))

    tiles = seq // SEQ_TILE
    seq_spec = lambda width: pl.BlockSpec((SEQ_TILE, width), lambda b, s: (b * tiles + s, 0))
    mix = pl.pallas_call(
        functools.partial(_mixer_kernel, seq_tile=SEQ_TILE),
        grid=(batch, tiles),
        in_specs=[seq_spec(3 * D_ATT), seq_spec(D_ATT), seq_spec(2 * D_GLA_K), seq_spec(D_GLA_V),
                  seq_spec(D_GLA_V), seq_spec(D_GLA_K),
                  _const_spec((N_HEAD_PAIRS, KEY_SPAN, 2 * PAIR_ROWS)),
                  _const_spec((N_GLA_HEADS, GLA_HEAD_V))],
        out_specs=seq_spec(d_mix),
        out_shape=jax.ShapeDtypeStruct((n, d_mix), _BF16),
        scratch_shapes=[pltpu.VMEM((LEFT + SEQ_TILE, D_ATT), _BF16),
                        pltpu.VMEM(((LEFT + SEQ_TILE) // KEY_BLOCK, D_ATT, KEY_BLOCK), _BF16),
                        pltpu.VMEM((GLA_HEAD_V, D_GLA_K), _F32)],
        compiler_params=cparams(dimension_semantics=("arbitrary", "arbitrary")),
        name="mixers",
    )(aqkv, ag, gqk, gv, gg, loga, bias, gla_norm_g[0].astype(_F32))

    out = pl.pallas_call(
        functools.partial(_epilogue_kernel, alpha=alpha),
        grid=(n // ROW_TILE,),
        in_specs=[row_spec(d_model), _const_spec((1, d_model)), _const_spec((1, d_model)),
                  row_spec(d_mix), row_spec(d_ple),
                  _const_spec((d_mix, d_model)), _const_spec((d_model, d_model)),
                  _const_spec((1, d_model)), _const_spec((d_ple, d_model)),
                  _const_spec((1, d_model)), _const_spec((1, d_model))],
        out_specs=row_spec(d_model),
        out_shape=jax.ShapeDtypeStruct((n, d_model), x.dtype),
        compiler_params=cparams(dimension_semantics=("parallel",)),
        name="out_proj_ple_norm",
    )(x2, row(ln_in_g), row(ln_in_b), mix, p2, w_out[0].astype(_BF16), w_ple_gate[0].astype(_BF16),
      row(b_ple_gate[0]), w_ple[0].astype(_BF16), row(ln_g[0]), row(ln_b[0]))
    return out.reshape(batch, seq, d_model)
```

```python
import functools
import math

import jax
import jax.numpy as jnp
from jax import lax
from jax.experimental import pallas as pl
from jax.experimental.pallas import tpu as pltpu

CHUNK = 64
N_ATT_HEADS = 8
ATT_HEAD_DIM = 64
D_ATT = N_ATT_HEADS * ATT_HEAD_DIM
LEFT_CHUNKS = 8
LEFT = LEFT_CHUNKS * CHUNK
BAND = LEFT + CHUNK
REL_CLIP = 128
N_GLA_HEADS = 4
GLA_HEAD_K = 64
GLA_HEAD_V = 128
D_GLA_K = N_GLA_HEADS * GLA_HEAD_K
D_GLA_V = N_GLA_HEADS * GLA_HEAD_V
GLA_LOW_RANK = 16
GLA_TAU = 16.0
LN_EPS = 1e-5
RMS_EPS = 1e-6
MASK_VALUE = -1e30
LOG2_E = math.log2(math.e)

LANES = 128
HEAD_PAIR = 2 * ATT_HEAD_DIM
assert HEAD_PAIR == LANES and 2 * GLA_HEAD_K == LANES
N_HEAD_PAIRS = N_ATT_HEADS // 2
PAIR_ROWS = 2 * CHUNK
KEY_SPAN = LEFT + PAIR_ROWS
KEY_BLOCK = LANES
N_SPAN_BLOCKS = KEY_SPAN // KEY_BLOCK
N_LEFT_BLOCKS = LEFT // KEY_BLOCK
assert PAIR_ROWS == KEY_BLOCK
BIAS_TOP = CHUNK
BIAS_BOTTOM = REL_CLIP + PAIR_ROWS
SUM_ROWS = 16

_OFF_AG = 3 * D_ATT
_OFF_GQK = 4 * D_ATT
_OFF_GV = _OFF_GQK + 2 * D_GLA_K
_OFF_GG = _OFF_GV + D_GLA_V
_OFF_LR = _OFF_GG + D_GLA_V
D_IN_PROJ = _OFF_LR + GLA_LOW_RANK
D_IN_PAD = _OFF_LR + LANES

ROW_TILE = 1024
SEQ_TILE = 512
VMEM_LIMIT = 56 * 1024 * 1024

_F32 = jnp.float32
_BF16 = jnp.bfloat16
_NT = (((1,), (1,)), ((), ()))
_TN = (((0,), (0,)), ((), ()))


def _layer_norm(xf, g, b):
    mu = jnp.mean(xf, axis=-1, keepdims=True)
    xc = xf - mu
    var = jnp.mean(xc * xc, axis=-1, keepdims=True)
    return xc * lax.rsqrt(var + LN_EPS) * g + b


def _silu(x):
    return x * jax.nn.sigmoid(x)


def _dot(a, b):
    return jnp.dot(a, b, preferred_element_type=_F32)


def _proj_kernel(x_ref, g_ref, b_ref, w_ref, wg_ref, bg_ref, gn_ref,
                 aqkv_ref, ag_ref, gqk_ref, gv_ref, gg_ref, loga_ref):
    hb = _layer_norm(x_ref[...], g_ref[...], b_ref[...]).astype(_BF16)
    aq = _dot(hb, w_ref[:, 0:D_ATT]) * (ATT_HEAD_DIM ** -0.5 * LOG2_E)
    aqkv_ref[:, 0:D_ATT] = aq.astype(_BF16)
    aqkv_ref[:, D_ATT:3 * D_ATT] = _dot(hb, w_ref[:, D_ATT:3 * D_ATT]).astype(_BF16)
    ag_ref[...] = _silu(_dot(hb, w_ref[:, _OFF_AG:_OFF_GQK]))
    gqk_ref[:, 0:D_GLA_K] = _dot(hb, w_ref[:, _OFF_GQK:_OFF_GQK + D_GLA_K]) * (GLA_HEAD_K ** -0.5)
    gqk_ref[:, D_GLA_K:] = _dot(hb, w_ref[:, _OFF_GQK + D_GLA_K:_OFF_GV])
    gv_ref[...] = _dot(hb, w_ref[:, _OFF_GV:_OFF_GG]).astype(_BF16)
    gg_ref[...] = _silu(_dot(hb, w_ref[:, _OFF_GG:_OFF_LR])) * gn_ref[...]
    code = _dot(hb, w_ref[:, _OFF_LR:D_IN_PAD]).astype(_BF16)
    logit = _dot(code, wg_ref[...]) + bg_ref[...]
    loga_ref[...] = jax.nn.log_sigmoid(logit) * (1.0 / GLA_TAU)


def _mixer_kernel(aqkv_ref, ag_ref, gqk_ref, gv_ref, gg_ref, loga_ref, bias_ref,
                  mix_ref, k_hist, vt_hist, state_ref, *, seq_tile):
    s = pl.program_id(1)
    n_new_blocks = seq_tile // KEY_BLOCK

    @pl.when(s == 0)
    def _():
        k_hist[0:LEFT, :] = jnp.zeros((LEFT, D_ATT), _BF16)
        vt_hist[0:N_LEFT_BLOCKS] = jnp.zeros((N_LEFT_BLOCKS, D_ATT, KEY_BLOCK), _BF16)
        state_ref[...] = jnp.zeros_like(state_ref)

    @pl.when(s > 0)
    def _():
        k_hist[0:LEFT, :] = k_hist[seq_tile:seq_tile + LEFT, :]
        vt_hist[0:N_LEFT_BLOCKS] = vt_hist[n_new_blocks:n_new_blocks + N_LEFT_BLOCKS]

    k_hist[LEFT:LEFT + seq_tile, :] = aqkv_ref[:, D_ATT:2 * D_ATT]
    for blk in range(n_new_blocks):
        v_blk = aqkv_ref[blk * KEY_BLOCK:(blk + 1) * KEY_BLOCK, 2 * D_ATT:3 * D_ATT]
        vt_hist[N_LEFT_BLOCKS + blk] = v_blk.T

    lane = lax.broadcasted_iota(jnp.int32, (1, LANES), 1)
    low_half = lane < ATT_HEAD_DIM
    top_rows = lax.broadcasted_iota(jnp.int32, (LANES, 1), 0) < ATT_HEAD_DIM
    row_i = lax.broadcasted_iota(jnp.int32, (CHUNK, CHUNK), 0)
    col_i = lax.broadcasted_iota(jnp.int32, (CHUNK, CHUNK), 1)
    causal = row_i >= col_i
    tril_ones = causal.astype(_BF16)
    ones_rows = jnp.ones((SUM_ROWS, KEY_SPAN), _BF16)

    def pair_body(cp, *, mask_padding):
        r0 = cp * PAIR_ROWS
        rows = pl.ds(r0, PAIR_ROWS)
        span = pl.ds(r0, KEY_SPAN)
        first_key = s * seq_tile + cp * PAIR_ROWS - LEFT

        def att_scores(j):
            cols = slice(j * LANES, (j + 1) * LANES)
            qp = aqkv_ref[rows, cols]
            zero = jnp.zeros_like(qp)
            q_blk = jnp.concatenate([jnp.where(low_half, qp, zero), jnp.where(low_half, zero, qp)], axis=0)
            kp = k_hist[span, cols]
            sc = lax.dot_general(kp, q_blk, _NT, preferred_element_type=_F32)
            slabs = [sc[0:BIAS_TOP] + bias_ref[j, 0:BIAS_TOP, :],
                     sc[BIAS_TOP:KEY_SPAN - BIAS_BOTTOM],
                     sc[KEY_SPAN - BIAS_BOTTOM:] + bias_ref[j, BIAS_TOP:, :]]
            sc = jnp.concatenate(slabs, axis=0)
            if not mask_padding:
                return sc
            blocks = [jnp.where(first_key + i * KEY_BLOCK >= 0, sc[i * KEY_BLOCK:(i + 1) * KEY_BLOCK, :], MASK_VALUE)
                      for i in range(N_SPAN_BLOCKS)]
            return jnp.concatenate(blocks, axis=0)

        def att_softmax(sc):
            m = jnp.max(sc, axis=0, keepdims=True)
            return jnp.exp2(sc - m).astype(_BF16)

        def att_out(j, pe):
            cols = slice(j * LANES, (j + 1) * LANES)
            vt = jnp.concatenate([vt_hist[cp + i, cols, :] for i in range(N_SPAN_BLOCKS)], axis=1)
            ot = _dot(jnp.concatenate([vt, ones_rows], axis=0), pe)
            inv = 1.0 / ot[LANES:LANES + 1, :]
            num = jnp.where(top_rows, ot[0:LANES, 0:LANES], ot[0:LANES, LANES:2 * LANES])
            scale = jnp.where(top_rows, inv[:, 0:LANES], inv[:, LANES:2 * LANES])
            att = (num * scale).T
            mix_ref[rows, cols] = (att * ag_ref[rows, cols]).astype(_BF16)

        def gla_cumsum(c):
            crow = pl.ds(cp * PAIR_ROWS + c * CHUNK, CHUNK)
            la = loga_ref[crow, :]
            la_hi = la.astype(_BF16)
            rem = la - la_hi.astype(_F32)
            la_mid = rem.astype(_BF16)
            la_lo = (rem - la_mid.astype(_F32)).astype(_BF16)
            return _dot(tril_ones, la_hi) + _dot(tril_ones, la_mid) + _dot(tril_ones, la_lo)

        def gla_scores(c, cum):
            crow = pl.ds(cp * PAIR_ROWS + c * CHUNK, CHUNK)
            cum_end = cum[CHUNK - 1:CHUNK, :]
            e_pos = jnp.exp(cum)
            e_neg = jnp.exp(-cum)
            gq = gqk_ref[crow, 0:D_GLA_K]
            gk = gqk_ref[crow, D_GLA_K:2 * D_GLA_K]
            q_fwd = gq * e_pos
            q_bwd = gq * e_neg
            k_fwd = (gk * e_pos).astype(_BF16)
            k_bwd = (gk * e_neg).astype(_BF16)
            k_end = (gk * jnp.exp(cum_end - cum)).astype(_BF16)
            state = state_ref[...]
            state_b = state.astype(_BF16)
            kv = []
            parts = []
            for h in range(N_GLA_HEADS):
                pcols = slice((h // 2) * LANES, (h // 2 + 1) * LANES)
                sel = low_half if h % 2 == 0 else jnp.logical_not(low_half)
                qf = jnp.where(sel, q_fwd[:, pcols], 0.0).astype(_BF16)
                qb = jnp.where(sel, q_bwd[:, pcols], 0.0).astype(_BF16)
                a_causal = lax.dot_general(qf, k_bwd[:, pcols], _NT, preferred_element_type=_F32)
                a_anti = lax.dot_general(qb, k_fwd[:, pcols], _NT, preferred_element_type=_F32)
                o_inter = lax.dot_general(qf, state_b[:, pcols], _NT, preferred_element_type=_F32)
                v_h = gv_ref[crow, h * GLA_HEAD_V:(h + 1) * GLA_HEAD_V]
                kv.append(lax.dot_general(v_h, k_end[:, pcols], _TN, preferred_element_type=_F32))
                parts.append((a_causal, a_anti, o_inter))
            new_state = jnp.concatenate(
                [jnp.where(low_half, kv[2 * g], kv[2 * g + 1]) for g in range(N_GLA_HEADS // 2)], axis=1)
            state_ref[...] = jnp.exp(cum_end) * state + new_state
            return parts

        def gla_out(c, parts):
            crow = pl.ds(cp * PAIR_ROWS + c * CHUNK, CHUNK)
            for h in range(N_GLA_HEADS):
                a_causal, a_anti, o_inter = parts[h]
                hv = slice(h * GLA_HEAD_V, (h + 1) * GLA_HEAD_V)
                att = jnp.where(causal, a_causal, a_anti).astype(_BF16)
                o = _dot(att, gv_ref[crow, hv]) + o_inter
                o = o * lax.rsqrt(jnp.mean(o * o, axis=-1, keepdims=True) + RMS_EPS)
                mix_ref[crow, D_ATT + h * GLA_HEAD_V:D_ATT + (h + 1) * GLA_HEAD_V] = (
                    o * gg_ref[crow, hv]).astype(_BF16)

        cum0 = gla_cumsum(0)
        cum1 = gla_cumsum(1)
        sc0 = att_scores(0)
        sc1 = att_scores(1)
        g0 = gla_scores(0, cum0)
        att_out(0, att_softmax(sc0))
        sc2 = att_scores(2)
        gla_out(0, g0)
        g1 = gla_scores(1, cum1)
        att_out(1, att_softmax(sc1))
        sc3 = att_scores(3)
        gla_out(1, g1)
        att_out(2, att_softmax(sc2))
        att_out(3, att_softmax(sc3))

    n_pairs = seq_tile // PAIR_ROWS

    @pl.when(s == 0)
    def _():
        for cp in range(n_pairs):
            pair_body(cp, mask_padding=True)

    @pl.when(s > 0)
    def _():
        for cp in range(n_pairs):
            pair_body(cp, mask_padding=False)


def _epilogue_kernel(x_ref, g_ref, b_ref, mix_ref, p_ref, wo_ref, wpg_ref, bpg_ref, wp_ref,
                     og_ref, ob_ref, out_ref, *, alpha):
    h = _layer_norm(x_ref[...], g_ref[...], b_ref[...])
    r = alpha * h + _dot(mix_ref[...], wo_ref[...])
    gate = jax.nn.sigmoid(_dot(r.astype(_BF16), wpg_ref[...]) + bpg_ref[...])
    r = r + gate * _dot(p_ref[...].astype(_BF16), wp_ref[...])
    out_ref[...] = _layer_norm(r, og_ref[...], ob_ref[...])


def _const_spec(shape):
    zeros = (0,) * len(shape)
    return pl.BlockSpec(shape, lambda *_: zeros, pipeline_mode=pl.Buffered(1))


def _rel_bias_span(rel_table):
    table = rel_table.astype(_F32)
    width = CHUNK - 1 + BAND
    ext_idx = jnp.clip(BAND - 1 - jnp.arange(width), -REL_CLIP, REL_CLIP) + REL_CLIP
    ext = table[:, ext_idx] - table[:, 2 * REL_CLIP:2 * REL_CLIP + 1]
    band = jnp.stack([ext[:, CHUNK - 1 - q:CHUNK - 1 - q + BAND] for q in range(CHUNK)], axis=1) * LOG2_E
    pad = lambda lo, hi: jnp.pad(band, ((0, 0), (0, 0), (lo, hi)), constant_values=MASK_VALUE)
    both = jnp.stack([pad(0, CHUNK), pad(CHUNK, 0)], axis=1)
    both = both.reshape(N_HEAD_PAIRS, 2, 2, CHUNK, KEY_SPAN)
    full = both.transpose(0, 4, 1, 2, 3).reshape(N_HEAD_PAIRS, KEY_SPAN, 2 * PAIR_ROWS)
    return jnp.concatenate([full[:, :BIAS_TOP], full[:, KEY_SPAN - BIAS_BOTTOM:]], axis=1)


def kernel(x, p, ln_in_g, ln_in_b, w_in, w_gla_gate, b_gla_gate, rel_bias,
           gla_norm_g, w_out, w_ple, w_ple_gate, b_ple_gate, ln_g, ln_b):
    batch, seq, d_model = x.shape
    depth = w_in.shape[0]
    assert depth == 1, "single-layer stack only"
    assert w_in.shape[2] == D_IN_PROJ
    assert seq % SEQ_TILE == 0 and SEQ_TILE % PAIR_ROWS == 0 and SEQ_TILE >= LEFT
    n = batch * seq
    assert n % ROW_TILE == 0
    d_ple = p.shape[-1]
    d_mix = D_ATT + D_GLA_V
    alpha = (2.0 * depth) ** 0.25

    x2 = x.reshape(n, d_model)
    p2 = p[0].reshape(n, d_ple)
    row = lambda v: v.reshape(1, -1).astype(_F32)
    w_in_b = jnp.pad(w_in[0], ((0, 0), (0, D_IN_PAD - D_IN_PROJ))).astype(_BF16)
    wg_b = jnp.pad(w_gla_gate[0], ((0, LANES - GLA_LOW_RANK), (0, 0))).astype(_BF16)
    bias = _rel_bias_span(rel_bias[0])

    cparams = functools.partial(pltpu.CompilerParams, vmem_limit_bytes=VMEM_LIMIT)
    row_spec = lambda width: pl.BlockSpec((ROW_TILE, width), lambda i: (i, 0))

    aqkv, ag, gqk, gv, gg, loga = pl.pallas_call(
        _proj_kernel,
        grid=(n // ROW_TILE,),
        in_specs=[row_spec(d_model), _const_spec((1, d_model)), _const_spec((1, d_model)),
                  _const_spec((d_model, D_IN_PAD)), _const_spec((LANES, D_GLA_K)),
                  _const_spec((1, D_GLA_K)), _const_spec((1, D_GLA_V))],
        out_specs=[row_spec(3 * D_ATT), row_spec(D_ATT), row_spec(2 * D_GLA_K),
                   row_spec(D_GLA_V), row_spec(D_GLA_V), row_spec(D_GLA_K)],
        out_shape=[jax.ShapeDtypeStruct((n, 3 * D_ATT), _BF16),
                   jax.ShapeDtypeStruct((n, D_ATT), _F32),
                   jax.ShapeDtypeStruct((n, 2 * D_GLA_K), _F32),
                   jax.ShapeDtypeStruct((n, D_GLA_V), _BF16),
                   jax.ShapeDtypeStruct((n, D_GLA_V), _F32),
                   jax.ShapeDtypeStruct((n, D_GLA_K), _F32)],
        compiler_params=cparams(dimension_semantics=("parallel",)),
        name="ln_in_proj",
    )(x2, row(ln_in_g), row(ln_in_b), w_in_b, wg_b, row(b_gla_gate[0]), row(gla_norm_g[0]))

    tiles = seq // SEQ_TILE
    seq_spec = lambda width: pl.BlockSpec((SEQ_TILE, width), lambda b, s: (b * tiles + s, 0))
    mix = pl.pallas_call(
        functools.partial(_mixer_kernel, seq_tile=SEQ_TILE),
        grid=(batch, tiles),
        in_specs=[seq_spec(3 * D_ATT), seq_spec(D_ATT), seq_spec(2 * D_GLA_K), seq_spec(D_GLA_V),
                  seq_spec(D_GLA_V), seq_spec(D_GLA_K),
                  _const_spec((N_HEAD_PAIRS, BIAS_TOP + BIAS_BOTTOM, 2 * PAIR_ROWS))],
        out_specs=seq_spec(d_mix),
        out_shape=jax.ShapeDtypeStruct((n, d_mix), _BF16),
        scratch_shapes=[pltpu.VMEM((LEFT + SEQ_TILE, D_ATT), _BF16),
                        pltpu.VMEM(((LEFT + SEQ_TILE) // KEY_BLOCK, D_ATT, KEY_BLOCK), _BF16),
                        pltpu.VMEM((GLA_HEAD_V, D_GLA_K), _F32)],
        compiler_params=cparams(dimension_semantics=("arbitrary", "arbitrary")),
        name="mixers",
    )(aqkv, ag, gqk, gv, gg, loga, bias)

    out = pl.pallas_call(
        functools.partial(_epilogue_kernel, alpha=alpha),
        grid=(n // ROW_TILE,),
        in_specs=[row_spec(d_model), _const_spec((1, d_model)), _const_spec((1, d_model)),
                  row_spec(d_mix), row_spec(d_ple),
                  _const_spec((d_mix, d_model)), _const_spec((d_model, d_model)),
                  _const_spec((1, d_model)), _const_spec((d_ple, d_model)),
                  _const_spec((1, d_model)), _const_spec((1, d_model))],
        out_specs=row_spec(d_model),
        out_shape=jax.ShapeDtypeStruct((n, d_model), x.dtype),
        compiler_params=cparams(dimension_semantics=("parallel",)),
        name="out_proj_ple_norm",
    )(x2, row(ln_in_g), row(ln_in_b), mix, p2, w_out[0].astype(_BF16), w_ple_gate[0].astype(_BF16),
      row(b_ple_gate[0]), w_ple[0].astype(_BF16), row(ln_g[0]), row(ln_b[0]))
    return out.reshape(batch, seq, d_model)
```

```python
import functools
import math

import jax
import jax.numpy as jnp
from jax import lax
from jax.experimental import pallas as pl
from jax.experimental.pallas import tpu as pltpu

CHUNK = 64
N_ATT_HEADS = 8
ATT_HEAD_DIM = 64
D_ATT = N_ATT_HEADS * ATT_HEAD_DIM
LEFT_CHUNKS = 8
LEFT = LEFT_CHUNKS * CHUNK
BAND = LEFT + CHUNK
REL_CLIP = 128
N_GLA_HEADS = 4
GLA_HEAD_K = 64
GLA_HEAD_V = 128
D_GLA_K = N_GLA_HEADS * GLA_HEAD_K
D_GLA_V = N_GLA_HEADS * GLA_HEAD_V
GLA_LOW_RANK = 16
GLA_TAU = 16.0
LN_EPS = 1e-5
RMS_EPS = 1e-6
MASK_VALUE = -1e30
LOG2_E = math.log2(math.e)

LANES = 128
HEAD_PAIR = 2 * ATT_HEAD_DIM
assert HEAD_PAIR == LANES and 2 * GLA_HEAD_K == LANES
N_HEAD_PAIRS = N_ATT_HEADS // 2
PAIR_ROWS = 2 * CHUNK
KEY_SPAN = LEFT + PAIR_ROWS
KEY_BLOCK = LANES
N_SPAN_BLOCKS = KEY_SPAN // KEY_BLOCK
N_LEFT_BLOCKS = LEFT // KEY_BLOCK
assert PAIR_ROWS == KEY_BLOCK
BIAS_TOP = CHUNK
BIAS_BOTTOM = REL_CLIP + PAIR_ROWS
SUM_ROWS = 16

_OFF_AG = 3 * D_ATT
_OFF_GQK = 4 * D_ATT
_OFF_GV = _OFF_GQK + 2 * D_GLA_K
_OFF_GG = _OFF_GV + D_GLA_V
_OFF_LR = _OFF_GG + D_GLA_V
D_IN_PROJ = _OFF_LR + GLA_LOW_RANK

ROW_TILE = 1024
SEQ_TILE = 512
VMEM_LIMIT = 56 * 1024 * 1024

_F32 = jnp.float32
_BF16 = jnp.bfloat16
_NT = (((1,), (1,)), ((), ()))
_TN = (((0,), (0,)), ((), ()))


def _layer_norm(xf, g, b):
    mu = jnp.mean(xf, axis=-1, keepdims=True)
    xc = xf - mu
    var = jnp.mean(xc * xc, axis=-1, keepdims=True)
    return xc * lax.rsqrt(var + LN_EPS) * g + b


def _silu(x):
    return x * jax.nn.sigmoid(x)


def _dot(a, b):
    return jnp.dot(a, b, preferred_element_type=_F32)


def _proj_kernel(x_ref, g_ref, b_ref, w_ref, wc_ref, wg_ref, bg_ref, gn_ref,
                 aqkv_ref, ag_ref, gqk_ref, gv_ref, gg_ref, loga_ref):
    hb = _layer_norm(x_ref[...], g_ref[...], b_ref[...]).astype(_BF16)
    aq = _dot(hb, w_ref[:, 0:D_ATT]) * (ATT_HEAD_DIM ** -0.5 * LOG2_E)
    aqkv_ref[:, 0:D_ATT] = aq.astype(_BF16)
    aqkv_ref[:, D_ATT:3 * D_ATT] = _dot(hb, w_ref[:, D_ATT:3 * D_ATT]).astype(_BF16)
    ag_ref[...] = _silu(_dot(hb, w_ref[:, _OFF_AG:_OFF_GQK]))
    gqk_ref[:, 0:D_GLA_K] = _dot(hb, w_ref[:, _OFF_GQK:_OFF_GQK + D_GLA_K]) * (GLA_HEAD_K ** -0.5)
    gqk_ref[:, D_GLA_K:] = _dot(hb, w_ref[:, _OFF_GQK + D_GLA_K:_OFF_GV])
    gv_ref[...] = _dot(hb, w_ref[:, _OFF_GV:_OFF_GG]).astype(_BF16)
    gg_ref[...] = _silu(_dot(hb, w_ref[:, _OFF_GG:_OFF_LR])) * gn_ref[...]
    code = _dot(hb, wc_ref[...]).astype(_BF16)
    logit = _dot(code, wg_ref[...]) + bg_ref[...]
    loga_ref[...] = jax.nn.log_sigmoid(logit) * (1.0 / GLA_TAU)


def _mixer_kernel(aqkv_ref, ag_ref, gqk_ref, gv_ref, gg_ref, loga_ref, bias_ref,
                  mix_ref, k_hist, vt_hist, state_ref, *, seq_tile):
    s = pl.program_id(1)
    n_new_blocks = seq_tile // KEY_BLOCK

    @pl.when(s == 0)
    def _():
        k_hist[0:LEFT, :] = jnp.zeros((LEFT, D_ATT), _BF16)
        vt_hist[0:N_LEFT_BLOCKS] = jnp.zeros((N_LEFT_BLOCKS, D_ATT, KEY_BLOCK), _BF16)
        state_ref[...] = jnp.zeros_like(state_ref)

    @pl.when(s > 0)
    def _():
        k_hist[0:LEFT, :] = k_hist[seq_tile:seq_tile + LEFT, :]
        vt_hist[0:N_LEFT_BLOCKS] = vt_hist[n_new_blocks:n_new_blocks + N_LEFT_BLOCKS]

    k_hist[LEFT:LEFT + seq_tile, :] = aqkv_ref[:, D_ATT:2 * D_ATT]
    for blk in range(n_new_blocks):
        v_blk = aqkv_ref[blk * KEY_BLOCK:(blk + 1) * KEY_BLOCK, 2 * D_ATT:3 * D_ATT]
        vt_hist[N_LEFT_BLOCKS + blk] = v_blk.T

    lane = lax.broadcasted_iota(jnp.int32, (1, LANES), 1)
    low_half = lane < ATT_HEAD_DIM
    top_rows = lax.broadcasted_iota(jnp.int32, (LANES, 1), 0) < ATT_HEAD_DIM
    row_i = lax.broadcasted_iota(jnp.int32, (CHUNK, CHUNK), 0)
    col_i = lax.broadcasted_iota(jnp.int32, (CHUNK, CHUNK), 1)
    causal = row_i >= col_i
    tril_ones = causal.astype(_BF16)
    ones_rows = jnp.ones((SUM_ROWS, KEY_SPAN), _BF16)

    def pair_body(cp, *, mask_padding):
        r0 = cp * PAIR_ROWS
        rows = pl.ds(r0, PAIR_ROWS)
        span = pl.ds(r0, KEY_SPAN)
        first_key = s * seq_tile + cp * PAIR_ROWS - LEFT

        def att_scores(j):
            cols = slice(j * LANES, (j + 1) * LANES)
            qp = aqkv_ref[rows, cols]
            zero = jnp.zeros_like(qp)
            q_blk = jnp.concatenate([jnp.where(low_half, qp, zero), jnp.where(low_half, zero, qp)], axis=0)
            kp = k_hist[span, cols]
            sc = lax.dot_general(kp, q_blk, _NT, preferred_element_type=_F32)
            slabs = [sc[0:BIAS_TOP] + bias_ref[j, 0:BIAS_TOP, :],
                     sc[BIAS_TOP:KEY_SPAN - BIAS_BOTTOM],
                     sc[KEY_SPAN - BIAS_BOTTOM:] + bias_ref[j, BIAS_TOP:, :]]
            sc = jnp.concatenate(slabs, axis=0)
            if not mask_padding:
                return sc
            blocks = [jnp.where(first_key + i * KEY_BLOCK >= 0, sc[i * KEY_BLOCK:(i + 1) * KEY_BLOCK, :], MASK_VALUE)
                      for i in range(N_SPAN_BLOCKS)]
            return jnp.concatenate(blocks, axis=0)

        def att_softmax(sc):
            m = jnp.max(sc, axis=0, keepdims=True)
            return jnp.exp2(sc - m).astype(_BF16)

        def att_out(j, pe):
            cols = slice(j * LANES, (j + 1) * LANES)
            vt = jnp.concatenate([vt_hist[cp + i, cols, :] for i in range(N_SPAN_BLOCKS)], axis=1)
            ot = _dot(jnp.concatenate([vt, ones_rows], axis=0), pe)
            inv = 1.0 / ot[LANES:LANES + 1, :]
            num = jnp.where(top_rows, ot[0:LANES, 0:LANES], ot[0:LANES, LANES:2 * LANES])
            scale = jnp.where(top_rows, inv[:, 0:LANES], inv[:, LANES:2 * LANES])
            att = (num * scale).T
            mix_ref[rows, cols] = (att * ag_ref[rows, cols]).astype(_BF16)

        def gla_cumsum(c):
            crow = pl.ds(cp * PAIR_ROWS + c * CHUNK, CHUNK)
            la = loga_ref[crow, :]
            la_hi = la.astype(_BF16)
            rem = la - la_hi.astype(_F32)
            la_mid = rem.astype(_BF16)
            la_lo = (rem - la_mid.astype(_F32)).astype(_BF16)
            return _dot(tril_ones, la_hi) + _dot(tril_ones, la_mid) + _dot(tril_ones, la_lo)

        def gla_scores(c, cum):
            crow = pl.ds(cp * PAIR_ROWS + c * CHUNK, CHUNK)
            cum_end = cum[CHUNK - 1:CHUNK, :]
            e_pos = jnp.exp(cum)
            e_neg = jnp.exp(-cum)
            gq = gqk_ref[crow, 0:D_GLA_K]
            gk = gqk_ref[crow, D_GLA_K:2 * D_GLA_K]
            q_fwd = gq * e_pos
            q_bwd = gq * e_neg
            k_fwd = (gk * e_pos).astype(_BF16)
            k_bwd = (gk * e_neg).astype(_BF16)
            k_end = (gk * jnp.exp(cum_end - cum)).astype(_BF16)
            state = state_ref[...]
            state_b = state.astype(_BF16)
            kv = []
            parts = []
            for h in range(N_GLA_HEADS):
                pcols = slice((h // 2) * LANES, (h // 2 + 1) * LANES)
                sel = low_half if h % 2 == 0 else jnp.logical_not(low_half)
                qf = jnp.where(sel, q_fwd[:, pcols], 0.0).astype(_BF16)
                qb = jnp.where(sel, q_bwd[:, pcols], 0.0).astype(_BF16)
                a_causal = lax.dot_general(qf, k_bwd[:, pcols], _NT, preferred_element_type=_F32)
                a_anti = lax.dot_general(qb, k_fwd[:, pcols], _NT, preferred_element_type=_F32)
                o_inter = lax.dot_general(qf, state_b[:, pcols], _NT, preferred_element_type=_F32)
                v_h = gv_ref[crow, h * GLA_HEAD_V:(h + 1) * GLA_HEAD_V]
                kv.append(lax.dot_general(v_h, k_end[:, pcols], _TN, preferred_element_type=_F32))
                parts.append((a_causal, a_anti, o_inter))
            new_state = jnp.concatenate(
                [jnp.where(low_half, kv[2 * g], kv[2 * g + 1]) for g in range(N_GLA_HEADS // 2)], axis=1)
            state_ref[...] = jnp.exp(cum_end) * state + new_state
            return parts

        def gla_out(c, parts):
            crow = pl.ds(cp * PAIR_ROWS + c * CHUNK, CHUNK)
            for h in range(N_GLA_HEADS):
                a_causal, a_anti, o_inter = parts[h]
                hv = slice(h * GLA_HEAD_V, (h + 1) * GLA_HEAD_V)
                att = jnp.where(causal, a_causal, a_anti).astype(_BF16)
                o = _dot(att, gv_ref[crow, hv]) + o_inter
                o = o * lax.rsqrt(jnp.mean(o * o, axis=-1, keepdims=True) + RMS_EPS)
                mix_ref[crow, D_ATT + h * GLA_HEAD_V:D_ATT + (h + 1) * GLA_HEAD_V] = (
                    o * gg_ref[crow, hv]).astype(_BF16)

        cum0 = gla_cumsum(0)
        cum1 = gla_cumsum(1)
        sc0 = att_scores(0)
        sc1 = att_scores(1)
        g0 = gla_scores(0, cum0)
        att_out(0, att_softmax(sc0))
        sc2 = att_scores(2)
        gla_out(0, g0)
        g1 = gla_scores(1, cum1)
        att_out(1, att_softmax(sc1))
        sc3 = att_scores(3)
        gla_out(1, g1)
        att_out(2, att_softmax(sc2))
        att_out(3, att_softmax(sc3))

    n_pairs = seq_tile // PAIR_ROWS

    @pl.when(s == 0)
    def _():
        for cp in range(n_pairs):
            pair_body(cp, mask_padding=True)

    @pl.when(s > 0)
    def _():
        for cp in range(n_pairs):
            pair_body(cp, mask_padding=False)


def _epilogue_kernel(x_ref, g_ref, b_ref, mix_ref, p_ref, wo_ref, wpg_ref, bpg_ref, wp_ref,
                     og_ref, ob_ref, out_ref, *, alpha):
    h = _layer_norm(x_ref[...], g_ref[...], b_ref[...])
    r = alpha * h + _dot(mix_ref[...], wo_ref[...])
    gate = jax.nn.sigmoid(_dot(r.astype(_BF16), wpg_ref[...]) + bpg_ref[...])
    r = r + gate * _dot(p_ref[...].astype(_BF16), wp_ref[...])
    out_ref[...] = _layer_norm(r, og_ref[...], ob_ref[...])


def _const_spec(shape):
    zeros = (0,) * len(shape)
    return pl.BlockSpec(shape, lambda *_: zeros, pipeline_mode=pl.Buffered(1))


def _rel_bias_span(rel_table):
    table = rel_table.astype(_F32)
    heads = table.shape[0]
    near = jnp.flip(table[:, REL_CLIP - (CHUNK - 1):2 * REL_CLIP], axis=1) - table[:, 2 * REL_CLIP:]
    ext = jnp.concatenate([jnp.zeros((heads, BAND - REL_CLIP), _F32), near], axis=1) * LOG2_E
    width = ext.shape[1] + 1
    tiled = jnp.tile(jnp.pad(ext, ((0, 0), (0, 1))), (1, CHUNK))[:, :CHUNK * (width - 1)]
    band = tiled.reshape(heads, CHUNK, width - 1)[:, :, CHUNK - 1:CHUNK - 1 + BAND]
    pad = lambda lo, hi: jnp.pad(band, ((0, 0), (0, 0), (lo, hi)), constant_values=MASK_VALUE)
    both = jnp.stack([pad(0, CHUNK), pad(CHUNK, 0)], axis=1)
    both = both.reshape(N_HEAD_PAIRS, 2, 2, CHUNK, KEY_SPAN)
    full = both.transpose(0, 4, 1, 2, 3).reshape(N_HEAD_PAIRS, KEY_SPAN, 2 * PAIR_ROWS)
    return jnp.concatenate([full[:, :BIAS_TOP], full[:, KEY_SPAN - BIAS_BOTTOM:]], axis=1)


def kernel(x, p, ln_in_g, ln_in_b, w_in, w_gla_gate, b_gla_gate, rel_bias,
           gla_norm_g, w_out, w_ple, w_ple_gate, b_ple_gate, ln_g, ln_b):
    batch, seq, d_model = x.shape
    depth = w_in.shape[0]
    assert depth == 1, "single-layer stack only"
    assert w_in.shape[2] == D_IN_PROJ
    assert seq % SEQ_TILE == 0 and SEQ_TILE % PAIR_ROWS == 0 and SEQ_TILE >= LEFT
    n = batch * seq
    assert n % ROW_TILE == 0
    d_ple = p.shape[-1]
    d_mix = D_ATT + D_GLA_V
    alpha = (2.0 * depth) ** 0.25

    x2 = x.reshape(n, d_model)
    p2 = p.reshape(n, d_ple)
    row = lambda v: v.reshape(1, -1).astype(_F32)
    w_main = w_in[0, :, :_OFF_LR].astype(_BF16)
    w_code = jnp.pad(w_in[0, :, _OFF_LR:], ((0, 0), (0, LANES - GLA_LOW_RANK))).astype(_BF16)
    wg_b = jnp.pad(w_gla_gate[0], ((0, LANES - GLA_LOW_RANK), (0, 0))).astype(_BF16)
    bias = _rel_bias_span(rel_bias[0])

    cparams = functools.partial(pltpu.CompilerParams, vmem_limit_bytes=VMEM_LIMIT)
    row_spec = lambda width: pl.BlockSpec((ROW_TILE, width), lambda i: (i, 0))

    aqkv, ag, gqk, gv, gg, loga = pl.pallas_call(
        _proj_kernel,
        grid=(n // ROW_TILE,),
        in_specs=[row_spec(d_model), _const_spec((1, d_model)), _const_spec((1, d_model)),
                  _const_spec((d_model, _OFF_LR)), _const_spec((d_model, LANES)),
                  _const_spec((LANES, D_GLA_K)),
                  _const_spec((1, D_GLA_K)), _const_spec((1, D_GLA_V))],
        out_specs=[row_spec(3 * D_ATT), row_spec(D_ATT), row_spec(2 * D_GLA_K),
                   row_spec(D_GLA_V), row_spec(D_GLA_V), row_spec(D_GLA_K)],
        out_shape=[jax.ShapeDtypeStruct((n, 3 * D_ATT), _BF16),
                   jax.ShapeDtypeStruct((n, D_ATT), _F32),
                   jax.ShapeDtypeStruct((n, 2 * D_GLA_K), _F32),
                   jax.ShapeDtypeStruct((n, D_GLA_V), _BF16),
                   jax.ShapeDtypeStruct((n, D_GLA_V), _F32),
                   jax.ShapeDtypeStruct((n, D_GLA_K), _F32)],
        compiler_params=cparams(dimension_semantics=("parallel",)),
        name="ln_in_proj",
    )(x2, row(ln_in_g), row(ln_in_b), w_main, w_code, wg_b, row(b_gla_gate[0]), row(gla_norm_g[0]))

    tiles = seq // SEQ_TILE
    seq_spec = lambda width: pl.BlockSpec((SEQ_TILE, width), lambda b, s: (b * tiles + s, 0))
    mix = pl.pallas_call(
        functools.partial(_mixer_kernel, seq_tile=SEQ_TILE),
        grid=(batch, tiles),
        in_specs=[seq_spec(3 * D_ATT), seq_spec(D_ATT), seq_spec(2 * D_GLA_K), seq_spec(D_GLA_V),
                  seq_spec(D_GLA_V), seq_spec(D_GLA_K),
                  _const_spec((N_HEAD_PAIRS, BIAS_TOP + BIAS_BOTTOM, 2 * PAIR_ROWS))],
        out_specs=seq_spec(d_mix),
        out_shape=jax.ShapeDtypeStruct((n, d_mix), _BF16),
        scratch_shapes=[pltpu.VMEM((LEFT + SEQ_TILE, D_ATT), _BF16),
                        pltpu.VMEM(((LEFT + SEQ_TILE) // KEY_BLOCK, D_ATT, KEY_BLOCK), _BF16),
                        pltpu.VMEM((GLA_HEAD_V, D_GLA_K), _F32)],
        compiler_params=cparams(dimension_semantics=("arbitrary", "arbitrary")),
        name="mixers",
    )(aqkv, ag, gqk, gv, gg, loga, bias)

    out = pl.pallas_call(
        functools.partial(_epilogue_kernel, alpha=alpha),
        grid=(n // ROW_TILE,),
        in_specs=[row_spec(d_model), _const_spec((1, d_model)), _const_spec((1, d_model)),
                  row_spec(d_mix), row_spec(d_ple),
                  _const_spec((d_mix, d_model)), _const_spec((d_model, d_model)),
                  _const_spec((1, d_model)), _const_spec((d_ple, d_model)),
                  _const_spec((1, d_model)), _const_spec((1, d_model))],
        out_specs=row_spec(d_model),
        out_shape=jax.ShapeDtypeStruct((n, d_model), x.dtype),
        compiler_params=cparams(dimension_semantics=("parallel",)),
        name="out_proj_ple_norm",
    )(x2, row(ln_in_g), row(ln_in_b), mix, p2, w_out[0].astype(_BF16), w_ple_gate[0].astype(_BF16),
      row(b_ple_gate[0]), w_ple[0].astype(_BF16), row(ln_g[0]), row(ln_b[0]))
    return out.reshape(batch, seq, d_model)
```

```python
import functools
import math

import jax
import jax.numpy as jnp
from jax import lax
from jax.experimental import pallas as pl
from jax.experimental.pallas import tpu as pltpu

CHUNK = 64
N_ATT_HEADS = 8
ATT_HEAD_DIM = 64
D_ATT = N_ATT_HEADS * ATT_HEAD_DIM
LEFT_CHUNKS = 8
LEFT = LEFT_CHUNKS * CHUNK
BAND = LEFT + CHUNK
REL_CLIP = 128
N_GLA_HEADS = 4
GLA_HEAD_K = 64
GLA_HEAD_V = 128
D_GLA_K = N_GLA_HEADS * GLA_HEAD_K
D_GLA_V = N_GLA_HEADS * GLA_HEAD_V
GLA_LOW_RANK = 16
GLA_TAU = 16.0
LN_EPS = 1e-5
RMS_EPS = 1e-6
MASK_VALUE = -1e30
LOG2_E = math.log2(math.e)

LANES = 128
HEAD_PAIR = 2 * ATT_HEAD_DIM
assert HEAD_PAIR == LANES and 2 * GLA_HEAD_K == LANES
N_HEAD_PAIRS = N_ATT_HEADS // 2
PAIR_ROWS = 2 * CHUNK
KEY_SPAN = LEFT + PAIR_ROWS
KEY_BLOCK = LANES
N_SPAN_BLOCKS = KEY_SPAN // KEY_BLOCK
N_LEFT_BLOCKS = LEFT // KEY_BLOCK
assert PAIR_ROWS == KEY_BLOCK
BIAS_TOP = CHUNK
BIAS_BOTTOM = REL_CLIP + PAIR_ROWS
SUM_ROWS = 16

_OFF_AG = 3 * D_ATT
_OFF_GQK = 4 * D_ATT
_OFF_GV = _OFF_GQK + 2 * D_GLA_K
_OFF_GG = _OFF_GV + D_GLA_V
_OFF_LR = _OFF_GG + D_GLA_V
D_IN_PROJ = _OFF_LR + GLA_LOW_RANK
PROJ_PIECE = 256
N_MAIN_PIECES = _OFF_LR // PROJ_PIECE
N_PROJ_PIECES = N_MAIN_PIECES + 1
PROJ_SLOTS_PER_PAIR = 4

ROW_TILE = 1024
SEQ_TILE = 512
VMEM_LIMIT = 56 * 1024 * 1024

_F32 = jnp.float32
_BF16 = jnp.bfloat16
_NT = (((1,), (1,)), ((), ()))
_TN = (((0,), (0,)), ((), ()))


def _layer_norm(xf, g, b):
    mu = jnp.mean(xf, axis=-1, keepdims=True)
    xc = xf - mu
    var = jnp.mean(xc * xc, axis=-1, keepdims=True)
    return xc * lax.rsqrt(var + LN_EPS) * g + b


def _silu(x):
    return x * jax.nn.sigmoid(x)


def _dot(a, b):
    return jnp.dot(a, b, preferred_element_type=_F32)


def _proj_mix_kernel(x0_ref, x_ref, g_ref, b_ref, w_ref, wc_ref, wg_ref, bg_ref, gn_ref, bias_ref,
                     mix_ref,
                     hb_ref, nhb_ref, nq_ref, nk_ref, nv_ref, nag_ref, ngqk_ref, ngv_ref, ngg_ref, nla_ref,
                     q_ref, ag_ref, gqk_ref, gv_ref, gg_ref, loga_ref,
                     k_hist, vt_hist, state_ref, *, seq_tile, tiles_per_seq):
    k = pl.program_id(0)
    s = lax.rem(k + tiles_per_seq - 1, tiles_per_seq)
    n_new_blocks = seq_tile // KEY_BLOCK
    next_refs = (nq_ref, nk_ref, nv_ref, nag_ref, ngqk_ref, ngv_ref, ngg_ref, nla_ref)

    @pl.when(k == 0)
    def _():
        for ref in next_refs + (k_hist, vt_hist, state_ref):
            ref[...] = jnp.zeros_like(ref)
        nhb_ref[...] = _layer_norm(x0_ref[...], g_ref[...], b_ref[...]).astype(_BF16)

    @pl.when(s == 0)
    def _():
        k_hist[0:LEFT, :] = jnp.zeros((LEFT, D_ATT), _BF16)
        vt_hist[0:N_LEFT_BLOCKS] = jnp.zeros((N_LEFT_BLOCKS, D_ATT, KEY_BLOCK), _BF16)
        state_ref[...] = jnp.zeros_like(state_ref)

    @pl.when(s > 0)
    def _():
        k_hist[0:LEFT, :] = k_hist[seq_tile:seq_tile + LEFT, :]
        vt_hist[0:N_LEFT_BLOCKS] = vt_hist[n_new_blocks:n_new_blocks + N_LEFT_BLOCKS]

    k_hist[LEFT:LEFT + seq_tile, :] = nk_ref[...]
    for blk in range(n_new_blocks):
        vt_hist[N_LEFT_BLOCKS + blk] = nv_ref[blk * KEY_BLOCK:(blk + 1) * KEY_BLOCK, :].T
    for dst, src in ((q_ref, nq_ref), (ag_ref, nag_ref), (gqk_ref, ngqk_ref), (gv_ref, ngv_ref),
                     (gg_ref, ngg_ref), (loga_ref, nla_ref)):
        dst[...] = src[...]

    hb_ref[...] = nhb_ref[...]

    def norm_next(cp):
        slab = pl.ds(cp * PAIR_ROWS, PAIR_ROWS)
        nhb_ref[slab, :] = _layer_norm(x_ref[slab, :], g_ref[...], b_ref[...]).astype(_BF16)

    def proj_piece(i):
        if i == N_MAIN_PIECES:
            code = _dot(hb_ref[...], wc_ref[...]).astype(_BF16)
            logit = _dot(code, wg_ref[...]) + bg_ref[...]
            nla_ref[...] = jax.nn.log_sigmoid(logit) * (1.0 / GLA_TAU)
            return
        c0 = i * PROJ_PIECE
        acc = _dot(hb_ref[...], w_ref[:, c0:c0 + PROJ_PIECE])

        def put(ref, base, val):
            ref[:, c0 - base:c0 - base + PROJ_PIECE] = val.astype(ref.dtype)

        if c0 < D_ATT:
            put(nq_ref, 0, acc * (ATT_HEAD_DIM ** -0.5 * LOG2_E))
        elif c0 < 2 * D_ATT:
            put(nk_ref, D_ATT, acc)
        elif c0 < _OFF_AG:
            put(nv_ref, 2 * D_ATT, acc)
        elif c0 < _OFF_GQK:
            put(nag_ref, _OFF_AG, _silu(acc))
        elif c0 < _OFF_GQK + D_GLA_K:
            put(ngqk_ref, _OFF_GQK, acc * (GLA_HEAD_K ** -0.5))
        elif c0 < _OFF_GV:
            put(ngqk_ref, _OFF_GQK, acc)
        elif c0 < _OFF_GG:
            put(ngv_ref, _OFF_GV, acc)
        else:
            put(ngg_ref, _OFF_GG, _silu(acc) * gn_ref[:, c0 - _OFF_GG:c0 - _OFF_GG + PROJ_PIECE])

    lane = lax.broadcasted_iota(jnp.int32, (1, LANES), 1)
    low_half = lane < ATT_HEAD_DIM
    top_rows = lax.broadcasted_iota(jnp.int32, (LANES, 1), 0) < ATT_HEAD_DIM
    row_i = lax.broadcasted_iota(jnp.int32, (CHUNK, CHUNK), 0)
    col_i = lax.broadcasted_iota(jnp.int32, (CHUNK, CHUNK), 1)
    causal = row_i >= col_i
    tril_ones = causal.astype(_BF16)
    ones_rows = jnp.ones((SUM_ROWS, KEY_SPAN), _BF16)

    def pair_body(cp, pieces):
        r0 = cp * PAIR_ROWS
        rows = pl.ds(r0, PAIR_ROWS)
        span = pl.ds(r0, KEY_SPAN)
        first_key = s * seq_tile + cp * PAIR_ROWS - LEFT
        pieces = list(pieces)

        def proj_next():
            piece = pieces.pop(0)
            if piece is not None:
                proj_piece(piece)

        def att_scores(j):
            cols = slice(j * LANES, (j + 1) * LANES)
            qp = q_ref[rows, cols]
            zero = jnp.zeros_like(qp)
            q_blk = jnp.concatenate([jnp.where(low_half, qp, zero), jnp.where(low_half, zero, qp)], axis=0)
            kp = k_hist[span, cols]
            sc = lax.dot_general(kp, q_blk, _NT, preferred_element_type=_F32)
            slabs = [sc[0:BIAS_TOP] + bias_ref[j, 0:BIAS_TOP, :],
                     sc[BIAS_TOP:KEY_SPAN - BIAS_BOTTOM],
                     sc[KEY_SPAN - BIAS_BOTTOM:] + bias_ref[j, BIAS_TOP:, :]]
            sc = jnp.concatenate(slabs, axis=0)
            blocks = [jnp.where(first_key + i * KEY_BLOCK >= 0, sc[i * KEY_BLOCK:(i + 1) * KEY_BLOCK, :], MASK_VALUE)
                      for i in range(N_SPAN_BLOCKS)]
            return jnp.concatenate(blocks, axis=0)

        def att_softmax(sc):
            m = jnp.max(sc, axis=0, keepdims=True)
            return jnp.exp2(sc - m).astype(_BF16)

        def att_out(j, pe):
            cols = slice(j * LANES, (j + 1) * LANES)
            vt = jnp.concatenate([vt_hist[cp + i, cols, :] for i in range(N_SPAN_BLOCKS)], axis=1)
            ot = _dot(jnp.concatenate([vt, ones_rows], axis=0), pe)
            inv = 1.0 / ot[LANES:LANES + 1, :]
            num = jnp.where(top_rows, ot[0:LANES, 0:LANES], ot[0:LANES, LANES:2 * LANES])
            scale = jnp.where(top_rows, inv[:, 0:LANES], inv[:, LANES:2 * LANES])
            att = (num * scale).T
            mix_ref[rows, cols] = (att * ag_ref[rows, cols]).astype(_BF16)

        def gla_cumsum(c):
            crow = pl.ds(cp * PAIR_ROWS + c * CHUNK, CHUNK)
            la = loga_ref[crow, :]
            la_hi = la.astype(_BF16)
            rem = la - la_hi.astype(_F32)
            la_mid = rem.astype(_BF16)
            la_lo = (rem - la_mid.astype(_F32)).astype(_BF16)
            return _dot(tril_ones, la_hi) + _dot(tril_ones, la_mid) + _dot(tril_ones, la_lo)

        def gla_scores(c, cum):
            crow = pl.ds(cp * PAIR_ROWS + c * CHUNK, CHUNK)
            cum_end = cum[CHUNK - 1:CHUNK, :]
            e_pos = jnp.exp(cum)
            e_neg = jnp.exp(-cum)
            gq = gqk_ref[crow, 0:D_GLA_K]
            gk = gqk_ref[crow, D_GLA_K:2 * D_GLA_K]
            q_fwd = gq * e_pos
            q_bwd = gq * e_neg
            k_fwd = (gk * e_pos).astype(_BF16)
            k_bwd = (gk * e_neg).astype(_BF16)
            k_end = (gk * jnp.exp(cum_end - cum)).astype(_BF16)
            state = state_ref[...]
            state_b = state.astype(_BF16)
            kv = []
            parts = []
            for h in range(N_GLA_HEADS):
                pcols = slice((h // 2) * LANES, (h // 2 + 1) * LANES)
                sel = low_half if h % 2 == 0 else jnp.logical_not(low_half)
                qf = jnp.where(sel, q_fwd[:, pcols], 0.0).astype(_BF16)
                qb = jnp.where(sel, q_bwd[:, pcols], 0.0).astype(_BF16)
                a_causal = lax.dot_general(qf, k_bwd[:, pcols], _NT, preferred_element_type=_F32)
                a_anti = lax.dot_general(qb, k_fwd[:, pcols], _NT, preferred_element_type=_F32)
                o_inter = lax.dot_general(qf, state_b[:, pcols], _NT, preferred_element_type=_F32)
                v_h = gv_ref[crow, h * GLA_HEAD_V:(h + 1) * GLA_HEAD_V]
                kv.append(lax.dot_general(v_h, k_end[:, pcols], _TN, preferred_element_type=_F32))
                parts.append((a_causal, a_anti, o_inter))
            new_state = jnp.concatenate(
                [jnp.where(low_half, kv[2 * g], kv[2 * g + 1]) for g in range(N_GLA_HEADS // 2)], axis=1)
            state_ref[...] = jnp.exp(cum_end) * state + new_state
            return parts

        def gla_out(c, parts):
            crow = pl.ds(cp * PAIR_ROWS + c * CHUNK, CHUNK)
            for h in range(N_GLA_HEADS):
                a_causal, a_anti, o_inter = parts[h]
                hv = slice(h * GLA_HEAD_V, (h + 1) * GLA_HEAD_V)
                att = jnp.where(causal, a_causal, a_anti).astype(_BF16)
                o = _dot(att, gv_ref[crow, hv]) + o_inter
                o = o * lax.rsqrt(jnp.mean(o * o, axis=-1, keepdims=True) + RMS_EPS)
                mix_ref[crow, D_ATT + h * GLA_HEAD_V:D_ATT + (h + 1) * GLA_HEAD_V] = (
                    o * gg_ref[crow, hv]).astype(_BF16)

        cum0 = gla_cumsum(0)
        cum1 = gla_cumsum(1)
        sc0 = att_scores(0)
        sc1 = att_scores(1)
        g0 = gla_scores(0, cum0)
        proj_next()
        att_out(0, att_softmax(sc0))
        sc2 = att_scores(2)
        gla_out(0, g0)
        g1 = gla_scores(1, cum1)
        proj_next()
        att_out(1, att_softmax(sc1))
        sc3 = att_scores(3)
        gla_out(1, g1)
        norm_next(cp)
        proj_next()
        att_out(2, att_softmax(sc2))
        proj_next()
        att_out(3, att_softmax(sc3))
        assert not pieces

    n_pairs = seq_tile // PAIR_ROWS
    slots = PROJ_SLOTS_PER_PAIR * n_pairs
    heavy = [N_MAIN_PIECES] + [c // PROJ_PIECE for c in range(_OFF_AG, _OFF_GQK, PROJ_PIECE)] \
        + [c // PROJ_PIECE for c in range(_OFF_GG, _OFF_LR, PROJ_PIECE)]
    light = [i for i in range(N_MAIN_PIECES) if i not in heavy]
    schedule = []
    for slot in range(slots):
        first = heavy if slot % PROJ_SLOTS_PER_PAIR == 0 else light
        source = first or heavy or light
        schedule.append(source.pop(0) if source else None)
    assert not heavy and not light
    for cp in range(n_pairs):
        pair_body(cp, schedule[cp * PROJ_SLOTS_PER_PAIR:(cp + 1) * PROJ_SLOTS_PER_PAIR])


def _epilogue_kernel(x_ref, g_ref, b_ref, mix_ref, p_ref, wo_ref, wpg_ref, bpg_ref, wp_ref,
                     og_ref, ob_ref, out_ref, *, alpha):
    h = _layer_norm(x_ref[...], g_ref[...], b_ref[...])
    r = alpha * h + _dot(mix_ref[...], wo_ref[...])
    gate = jax.nn.sigmoid(_dot(r.astype(_BF16), wpg_ref[...]) + bpg_ref[...])
    r = r + gate * _dot(p_ref[...].astype(_BF16), wp_ref[...])
    out_ref[...] = _layer_norm(r, og_ref[...], ob_ref[...])


def _const_spec(shape):
    zeros = (0,) * len(shape)
    return pl.BlockSpec(shape, lambda *_: zeros, pipeline_mode=pl.Buffered(1))


def _rel_bias_span(rel_table):
    table = rel_table.astype(_F32)
    heads = table.shape[0]
    near = jnp.flip(table[:, REL_CLIP - (CHUNK - 1):2 * REL_CLIP], axis=1) - table[:, 2 * REL_CLIP:]
    ext = jnp.concatenate([jnp.zeros((heads, BAND - REL_CLIP), _F32), near], axis=1) * LOG2_E
    width = ext.shape[1] + 1
    tiled = jnp.tile(jnp.pad(ext, ((0, 0), (0, 1))), (1, CHUNK))[:, :CHUNK * (width - 1)]
    band = tiled.reshape(heads, CHUNK, width - 1)[:, :, CHUNK - 1:CHUNK - 1 + BAND]
    pad = lambda lo, hi: jnp.pad(band, ((0, 0), (0, 0), (lo, hi)), constant_values=MASK_VALUE)
    both = jnp.stack([pad(0, CHUNK), pad(CHUNK, 0)], axis=1)
    both = both.reshape(N_HEAD_PAIRS, 2, 2, CHUNK, KEY_SPAN)
    full = both.transpose(0, 4, 1, 2, 3).reshape(N_HEAD_PAIRS, KEY_SPAN, 2 * PAIR_ROWS)
    return jnp.concatenate([full[:, :BIAS_TOP], full[:, KEY_SPAN - BIAS_BOTTOM:]], axis=1)


def kernel(x, p, ln_in_g, ln_in_b, w_in, w_gla_gate, b_gla_gate, rel_bias,
           gla_norm_g, w_out, w_ple, w_ple_gate, b_ple_gate, ln_g, ln_b):
    batch, seq, d_model = x.shape
    depth = w_in.shape[0]
    assert depth == 1, "single-layer stack only"
    assert w_in.shape[2] == D_IN_PROJ
    assert seq % SEQ_TILE == 0 and SEQ_TILE % PAIR_ROWS == 0 and SEQ_TILE >= LEFT
    n = batch * seq
    assert n % ROW_TILE == 0
    d_ple = p.shape[-1]
    d_mix = D_ATT + D_GLA_V
    alpha = (2.0 * depth) ** 0.25

    x2 = x.reshape(n, d_model)
    p2 = p.reshape(n, d_ple)
    row = lambda v: v.reshape(1, -1).astype(_F32)
    w_main = w_in[0, :, :_OFF_LR].astype(_BF16)
    w_code = jnp.pad(w_in[0, :, _OFF_LR:], ((0, 0), (0, LANES - GLA_LOW_RANK))).astype(_BF16)
    wg_b = jnp.pad(w_gla_gate[0], ((0, LANES - GLA_LOW_RANK), (0, 0))).astype(_BF16)
    bias = _rel_bias_span(rel_bias[0])

    cparams = functools.partial(pltpu.CompilerParams, vmem_limit_bytes=VMEM_LIMIT)

    tiles_per_seq = seq // SEQ_TILE
    n_tiles = batch * tiles_per_seq
    tile_f32 = lambda width: pltpu.VMEM((SEQ_TILE, width), _F32)
    tile_bf16 = lambda width: pltpu.VMEM((SEQ_TILE, width), _BF16)
    projected = [tile_bf16(D_ATT), tile_f32(D_ATT), tile_f32(2 * D_GLA_K), tile_bf16(D_GLA_V),
                 tile_f32(D_GLA_V), tile_f32(D_GLA_K)]
    mix = pl.pallas_call(
        functools.partial(_proj_mix_kernel, seq_tile=SEQ_TILE, tiles_per_seq=tiles_per_seq),
        grid=(n_tiles + 1,),
        in_specs=[_const_spec((SEQ_TILE, d_model)),
                  pl.BlockSpec((SEQ_TILE, d_model), lambda k: (jnp.minimum(k + 1, n_tiles - 1), 0)),
                  _const_spec((1, d_model)), _const_spec((1, d_model)),
                  _const_spec((d_model, _OFF_LR)), _const_spec((d_model, LANES)),
                  _const_spec((LANES, D_GLA_K)), _const_spec((1, D_GLA_K)), _const_spec((1, D_GLA_V)),
                  _const_spec((N_HEAD_PAIRS, BIAS_TOP + BIAS_BOTTOM, 2 * PAIR_ROWS))],
        out_specs=pl.BlockSpec((SEQ_TILE, d_mix), lambda k: (jnp.maximum(k - 1, 0), 0)),
        out_shape=jax.ShapeDtypeStruct((n, d_mix), _BF16),
        scratch_shapes=[tile_bf16(d_model), tile_bf16(d_model),
                        tile_bf16(D_ATT), tile_bf16(D_ATT), tile_bf16(D_ATT),
                        tile_f32(D_ATT), tile_f32(2 * D_GLA_K), tile_bf16(D_GLA_V),
                        tile_f32(D_GLA_V), tile_f32(D_GLA_K)]
                       + projected
                       + [pltpu.VMEM((LEFT + SEQ_TILE, D_ATT), _BF16),
                          pltpu.VMEM(((LEFT + SEQ_TILE) // KEY_BLOCK, D_ATT, KEY_BLOCK), _BF16),
                          pltpu.VMEM((GLA_HEAD_V, D_GLA_K), _F32)],
        compiler_params=cparams(dimension_semantics=("arbitrary",)),
        name="proj_mixers",
    )(x2, x2, row(ln_in_g), row(ln_in_b), w_main, w_code, wg_b, row(b_gla_gate[0]), row(gla_norm_g[0]), bias)

    row_spec = lambda width: pl.BlockSpec((ROW_TILE, width), lambda i: (i, 0))
    out = pl.pallas_call(
        functools.partial(_epilogue_kernel, alpha=alpha),
        grid=(n // ROW_TILE,),
        in_specs=[row_spec(d_model), _const_spec((1, d_model)), _const_spec((1, d_model)),
                  row_spec(d_mix), row_spec(d_ple),
                  _const_spec((d_mix, d_model)), _const_spec((d_model, d_model)),
                  _const_spec((1, d_model)), _const_spec((d_ple, d_model)),
                  _const_spec((1, d_model)), _const_spec((1, d_model))],
        out_specs=row_spec(d_model),
        out_shape=jax.ShapeDtypeStruct((n, d_model), x.dtype),
        compiler_params=cparams(dimension_semantics=("parallel",)),
        name="out_proj_ple_norm",
    )(x2, row(ln_in_g), row(ln_in_b), mix, p2, w_out[0].astype(_BF16), w_ple_gate[0].astype(_BF16),
      row(b_ple_gate[0]), w_ple[0].astype(_BF16), row(ln_g[0]), row(ln_b[0]))
    return out.reshape(batch, seq, d_model)
```

```python
import functools
import math

import jax
import jax.numpy as jnp
from jax import lax
from jax.experimental import pallas as pl
from jax.experimental.pallas import tpu as pltpu

CHUNK = 64
N_ATT_HEADS = 8
ATT_HEAD_DIM = 64
D_ATT = N_ATT_HEADS * ATT_HEAD_DIM
LEFT_CHUNKS = 8
LEFT = LEFT_CHUNKS * CHUNK
BAND = LEFT + CHUNK
REL_CLIP = 128
N_GLA_HEADS = 4
GLA_HEAD_K = 64
GLA_HEAD_V = 128
D_GLA_K = N_GLA_HEADS * GLA_HEAD_K
D_GLA_V = N_GLA_HEADS * GLA_HEAD_V
GLA_LOW_RANK = 16
GLA_TAU = 16.0
LN_EPS = 1e-5
RMS_EPS = 1e-6
MASK_VALUE = -1e30
LOG2_E = math.log2(math.e)

LANES = 128
HEAD_PAIR = 2 * ATT_HEAD_DIM
assert HEAD_PAIR == LANES and 2 * GLA_HEAD_K == LANES
N_HEAD_PAIRS = N_ATT_HEADS // 2
PAIR_ROWS = 2 * CHUNK
KEY_SPAN = LEFT + PAIR_ROWS
KEY_BLOCK = LANES
N_SPAN_BLOCKS = KEY_SPAN // KEY_BLOCK
N_LEFT_BLOCKS = LEFT // KEY_BLOCK
assert PAIR_ROWS == KEY_BLOCK
BIAS_TOP = CHUNK
BIAS_BOTTOM = REL_CLIP + PAIR_ROWS
SUM_ROWS = 16

_OFF_AG = 3 * D_ATT
_OFF_GQK = 4 * D_ATT
_OFF_GV = _OFF_GQK + 2 * D_GLA_K
_OFF_GG = _OFF_GV + D_GLA_V
_OFF_LR = _OFF_GG + D_GLA_V
D_IN_PROJ = _OFF_LR + GLA_LOW_RANK
PROJ_PIECE = 256
N_MAIN_PIECES = _OFF_LR // PROJ_PIECE
N_PROJ_PIECES = N_MAIN_PIECES + 1
PROJ_SLOTS_PER_PAIR = 4

ROW_TILE = 1024
EPILOGUE_SPLIT = 4
SEQ_TILE = 512
VMEM_LIMIT = 56 * 1024 * 1024

_F32 = jnp.float32
_BF16 = jnp.bfloat16
_NT = (((1,), (1,)), ((), ()))
_TN = (((0,), (0,)), ((), ()))


def _layer_norm(xf, g, b):
    mu = jnp.mean(xf, axis=-1, keepdims=True)
    xc = xf - mu
    var = jnp.mean(xc * xc, axis=-1, keepdims=True)
    return xc * lax.rsqrt(var + LN_EPS) * g + b


def _silu(x):
    return x * jax.nn.sigmoid(x)


def _dot(a, b):
    return jnp.dot(a, b, preferred_element_type=_F32)


def _proj_mix_kernel(x0_ref, x_ref, g_ref, b_ref, w_ref, wc_ref, wg_ref, bg_ref, gn_ref, bias_ref,
                     mix_ref,
                     hb_ref, nhb_ref, nq_ref, nk_ref, nv_ref, nag_ref, ngqk_ref, ngv_ref, ngg_ref, nla_ref,
                     q_ref, ag_ref, gqk_ref, gv_ref, gg_ref, loga_ref,
                     k_hist, vt_hist, state_ref, *, seq_tile, tiles_per_seq):
    k = pl.program_id(0)
    s = lax.rem(k + tiles_per_seq - 1, tiles_per_seq)
    n_new_blocks = seq_tile // KEY_BLOCK
    next_refs = (nq_ref, nk_ref, nv_ref, nag_ref, ngqk_ref, ngv_ref, ngg_ref, nla_ref)

    @pl.when(k == 0)
    def _():
        for ref in next_refs + (k_hist, vt_hist, state_ref):
            ref[...] = jnp.zeros_like(ref)
        nhb_ref[...] = _layer_norm(x0_ref[...], g_ref[...], b_ref[...]).astype(_BF16)

    @pl.when(s == 0)
    def _():
        k_hist[0:LEFT, :] = jnp.zeros((LEFT, D_ATT), _BF16)
        vt_hist[0:N_LEFT_BLOCKS] = jnp.zeros((N_LEFT_BLOCKS, D_ATT, KEY_BLOCK), _BF16)
        state_ref[...] = jnp.zeros_like(state_ref)

    @pl.when(s > 0)
    def _():
        k_hist[0:LEFT, :] = k_hist[seq_tile:seq_tile + LEFT, :]
        vt_hist[0:N_LEFT_BLOCKS] = vt_hist[n_new_blocks:n_new_blocks + N_LEFT_BLOCKS]

    k_hist[LEFT:LEFT + seq_tile, :] = nk_ref[...]
    for blk in range(n_new_blocks):
        vt_hist[N_LEFT_BLOCKS + blk] = nv_ref[blk * KEY_BLOCK:(blk + 1) * KEY_BLOCK, :].T
    for dst, src in ((q_ref, nq_ref), (ag_ref, nag_ref), (gqk_ref, ngqk_ref), (gv_ref, ngv_ref),
                     (gg_ref, ngg_ref), (loga_ref, nla_ref)):
        dst[...] = src[...]

    hb_ref[...] = nhb_ref[...]

    def norm_next(cp):
        slab = pl.ds(cp * PAIR_ROWS, PAIR_ROWS)
        nhb_ref[slab, :] = _layer_norm(x_ref[slab, :], g_ref[...], b_ref[...]).astype(_BF16)

    def proj_piece(i):
        if i == N_MAIN_PIECES:
            code = _dot(hb_ref[...], wc_ref[...]).astype(_BF16)
            logit = _dot(code, wg_ref[...]) + bg_ref[...]
            nla_ref[...] = jax.nn.log_sigmoid(logit) * (1.0 / GLA_TAU)
            return
        c0 = i * PROJ_PIECE
        acc = _dot(hb_ref[...], w_ref[:, c0:c0 + PROJ_PIECE])

        def put(ref, base, val):
            ref[:, c0 - base:c0 - base + PROJ_PIECE] = val.astype(ref.dtype)

        if c0 < D_ATT:
            put(nq_ref, 0, acc * (ATT_HEAD_DIM ** -0.5 * LOG2_E))
        elif c0 < 2 * D_ATT:
            put(nk_ref, D_ATT, acc)
        elif c0 < _OFF_AG:
            put(nv_ref, 2 * D_ATT, acc)
        elif c0 < _OFF_GQK:
            put(nag_ref, _OFF_AG, _silu(acc))
        elif c0 < _OFF_GQK + D_GLA_K:
            put(ngqk_ref, _OFF_GQK, acc * (GLA_HEAD_K ** -0.5))
        elif c0 < _OFF_GV:
            put(ngqk_ref, _OFF_GQK, acc)
        elif c0 < _OFF_GG:
            put(ngv_ref, _OFF_GV, acc)
        else:
            put(ngg_ref, _OFF_GG, _silu(acc) * gn_ref[:, c0 - _OFF_GG:c0 - _OFF_GG + PROJ_PIECE])

    lane = lax.broadcasted_iota(jnp.int32, (1, LANES), 1)
    low_half = lane < ATT_HEAD_DIM
    top_rows = lax.broadcasted_iota(jnp.int32, (LANES, 1), 0) < ATT_HEAD_DIM
    row_i = lax.broadcasted_iota(jnp.int32, (CHUNK, CHUNK), 0)
    col_i = lax.broadcasted_iota(jnp.int32, (CHUNK, CHUNK), 1)
    causal = row_i >= col_i
    chunk_row = lax.broadcasted_iota(jnp.int32, (CHUNK, 1), 0)
    ones_rows = jnp.ones((SUM_ROWS, KEY_SPAN), _BF16)

    def pair_body(cp, pieces):
        r0 = cp * PAIR_ROWS
        rows = pl.ds(r0, PAIR_ROWS)
        span = pl.ds(r0, KEY_SPAN)
        first_key = s * seq_tile + cp * PAIR_ROWS - LEFT
        pieces = list(pieces)

        def proj_next():
            piece = pieces.pop(0)
            if piece is not None:
                proj_piece(piece)

        def att_scores(j):
            cols = slice(j * LANES, (j + 1) * LANES)
            qp = q_ref[rows, cols]
            zero = jnp.zeros_like(qp)
            q_blk = jnp.concatenate([jnp.where(low_half, qp, zero), jnp.where(low_half, zero, qp)], axis=0)
            kp = k_hist[span, cols]
            sc = lax.dot_general(kp, q_blk, _NT, preferred_element_type=_F32)
            slabs = [sc[0:BIAS_TOP] + bias_ref[j, 0:BIAS_TOP, :],
                     sc[BIAS_TOP:KEY_SPAN - BIAS_BOTTOM],
                     sc[KEY_SPAN - BIAS_BOTTOM:] + bias_ref[j, BIAS_TOP:, :]]
            sc = jnp.concatenate(slabs, axis=0)
            blocks = [jnp.where(first_key + i * KEY_BLOCK >= 0, sc[i * KEY_BLOCK:(i + 1) * KEY_BLOCK, :], MASK_VALUE)
                      for i in range(N_SPAN_BLOCKS)]
            return jnp.concatenate(blocks, axis=0)

        def att_softmax(sc):
            m = jnp.max(sc, axis=0, keepdims=True)
            return jnp.exp2(sc - m).astype(_BF16)

        def att_out(j, pe):
            cols = slice(j * LANES, (j + 1) * LANES)
            vt = jnp.concatenate([vt_hist[cp + i, cols, :] for i in range(N_SPAN_BLOCKS)], axis=1)
            ot = _dot(jnp.concatenate([vt, ones_rows], axis=0), pe)
            inv = 1.0 / ot[LANES:LANES + 1, :]
            num = jnp.where(top_rows, ot[0:LANES, 0:LANES], ot[0:LANES, LANES:2 * LANES])
            scale = jnp.where(top_rows, inv[:, 0:LANES], inv[:, LANES:2 * LANES])
            att = (num * scale).T
            mix_ref[rows, cols] = (att * ag_ref[rows, cols]).astype(_BF16)

        def gla_cumsum(c):
            crow = pl.ds(cp * PAIR_ROWS + c * CHUNK, CHUNK)
            cum = loga_ref[crow, :]
            shift = 1
            while shift < CHUNK:
                cum = cum + jnp.where(chunk_row >= shift, pltpu.roll(cum, shift, axis=0), 0.0)
                shift *= 2
            return cum

        def gla_scores(c, cum):
            crow = pl.ds(cp * PAIR_ROWS + c * CHUNK, CHUNK)
            cum_end = cum[CHUNK - 1:CHUNK, :]
            e_pos = jnp.exp(cum)
            e_neg = jnp.exp(-cum)
            gq = gqk_ref[crow, 0:D_GLA_K]
            gk = gqk_ref[crow, D_GLA_K:2 * D_GLA_K]
            q_fwd = gq * e_pos
            q_bwd = gq * e_neg
            k_fwd = (gk * e_pos).astype(_BF16)
            k_bwd = (gk * e_neg).astype(_BF16)
            k_end = (gk * jnp.exp(cum_end - cum)).astype(_BF16)
            state = state_ref[...]
            state_b = state.astype(_BF16)
            kv = []
            parts = []
            for h in range(N_GLA_HEADS):
                pcols = slice((h // 2) * LANES, (h // 2 + 1) * LANES)
                sel = low_half if h % 2 == 0 else jnp.logical_not(low_half)
                qf = jnp.where(sel, q_fwd[:, pcols], 0.0).astype(_BF16)
                qb = jnp.where(sel, q_bwd[:, pcols], 0.0).astype(_BF16)
                a_causal = lax.dot_general(qf, k_bwd[:, pcols], _NT, preferred_element_type=_F32)
                a_anti = lax.dot_general(qb, k_fwd[:, pcols], _NT, preferred_element_type=_F32)
                o_inter = lax.dot_general(qf, state_b[:, pcols], _NT, preferred_element_type=_F32)
                v_h = gv_ref[crow, h * GLA_HEAD_V:(h + 1) * GLA_HEAD_V]
                kv.append(lax.dot_general(v_h, k_end[:, pcols], _TN, preferred_element_type=_F32))
                parts.append((a_causal, a_anti, o_inter))
            new_state = jnp.concatenate(
                [jnp.where(low_half, kv[2 * g], kv[2 * g + 1]) for g in range(N_GLA_HEADS // 2)], axis=1)
            state_ref[...] = jnp.exp(cum_end) * state + new_state
            return parts

        def gla_out(c, parts):
            crow = pl.ds(cp * PAIR_ROWS + c * CHUNK, CHUNK)
            for h in range(N_GLA_HEADS):
                a_causal, a_anti, o_inter = parts[h]
                hv = slice(h * GLA_HEAD_V, (h + 1) * GLA_HEAD_V)
                att = jnp.where(causal, a_causal, a_anti).astype(_BF16)
                o = _dot(att, gv_ref[crow, hv]) + o_inter
                o = o * lax.rsqrt(jnp.mean(o * o, axis=-1, keepdims=True) + RMS_EPS)
                mix_ref[crow, D_ATT + h * GLA_HEAD_V:D_ATT + (h + 1) * GLA_HEAD_V] = (
                    o * gg_ref[crow, hv]).astype(_BF16)

        cum0 = gla_cumsum(0)
        cum1 = gla_cumsum(1)
        sc0 = att_scores(0)
        sc1 = att_scores(1)
        g0 = gla_scores(0, cum0)
        proj_next()
        att_out(0, att_softmax(sc0))
        sc2 = att_scores(2)
        gla_out(0, g0)
        g1 = gla_scores(1, cum1)
        proj_next()
        att_out(1, att_softmax(sc1))
        sc3 = att_scores(3)
        gla_out(1, g1)
        norm_next(cp)
        proj_next()
        att_out(2, att_softmax(sc2))
        proj_next()
        att_out(3, att_softmax(sc3))
        assert not pieces

    n_pairs = seq_tile // PAIR_ROWS
    slots = PROJ_SLOTS_PER_PAIR * n_pairs
    heavy = [N_MAIN_PIECES] + [c // PROJ_PIECE for c in range(_OFF_AG, _OFF_GQK, PROJ_PIECE)] \
        + [c // PROJ_PIECE for c in range(_OFF_GG, _OFF_LR, PROJ_PIECE)]
    light = [i for i in range(N_MAIN_PIECES) if i not in heavy]
    schedule = []
    for slot in range(slots):
        first = heavy if slot % PROJ_SLOTS_PER_PAIR == 0 else light
        source = first or heavy or light
        schedule.append(source.pop(0) if source else None)
    assert not heavy and not light
    for cp in range(n_pairs):
        pair_body(cp, schedule[cp * PROJ_SLOTS_PER_PAIR:(cp + 1) * PROJ_SLOTS_PER_PAIR])


def _epilogue_kernel(x_ref, g_ref, b_ref, mix_ref, p_ref, wo_ref, wpg_ref, bpg_ref, wp_ref,
                     og_ref, ob_ref, out_ref, *, alpha):
    half = x_ref.shape[0] // EPILOGUE_SPLIT
    halves = [pl.ds(i * half, half) for i in range(EPILOGUE_SPLIT)]

    def residual(rows):
        h = _layer_norm(x_ref[rows, :], g_ref[...], b_ref[...])
        return alpha * h + _dot(mix_ref[rows, :], wo_ref[...])

    def gate_and_embed(rows, r):
        logit = _dot(r.astype(_BF16), wpg_ref[...])
        return logit, _dot(p_ref[rows, :].astype(_BF16), wp_ref[...])

    def finish(rows, r, logit, ple):
        r = r + jax.nn.sigmoid(logit + bpg_ref[...]) * ple
        out_ref[rows, :] = _layer_norm(r, og_ref[...], ob_ref[...])

    rs = [residual(rows) for rows in halves]
    ts = [gate_and_embed(rows, r) for rows, r in zip(halves, rs)]
    for rows, r, t in zip(halves, rs, ts):
        finish(rows, r, *t)


def _const_spec(shape):
    zeros = (0,) * len(shape)
    return pl.BlockSpec(shape, lambda *_: zeros, pipeline_mode=pl.Buffered(1))


def _rel_bias_span(rel_table):
    table = rel_table.astype(_F32)
    heads = table.shape[0]
    near = jnp.flip(table[:, REL_CLIP - (CHUNK - 1):2 * REL_CLIP], axis=1) - table[:, 2 * REL_CLIP:]
    ext = jnp.concatenate([jnp.zeros((heads, BAND - REL_CLIP), _F32), near], axis=1) * LOG2_E
    width = ext.shape[1] + 1
    tiled = jnp.tile(jnp.pad(ext, ((0, 0), (0, 1))), (1, CHUNK))[:, :CHUNK * (width - 1)]
    band = tiled.reshape(heads, CHUNK, width - 1)[:, :, CHUNK - 1:CHUNK - 1 + BAND]
    pad = lambda lo, hi: jnp.pad(band, ((0, 0), (0, 0), (lo, hi)), constant_values=MASK_VALUE)
    both = jnp.stack([pad(0, CHUNK), pad(CHUNK, 0)], axis=1)
    both = both.reshape(N_HEAD_PAIRS, 2, 2, CHUNK, KEY_SPAN)
    full = both.transpose(0, 4, 1, 2, 3).reshape(N_HEAD_PAIRS, KEY_SPAN, 2 * PAIR_ROWS)
    return jnp.concatenate([full[:, :BIAS_TOP], full[:, KEY_SPAN - BIAS_BOTTOM:]], axis=1)


def kernel(x, p, ln_in_g, ln_in_b, w_in, w_gla_gate, b_gla_gate, rel_bias,
           gla_norm_g, w_out, w_ple, w_ple_gate, b_ple_gate, ln_g, ln_b):
    batch, seq, d_model = x.shape
    depth = w_in.shape[0]
    assert depth == 1, "single-layer stack only"
    assert w_in.shape[2] == D_IN_PROJ
    assert seq % SEQ_TILE == 0 and SEQ_TILE % PAIR_ROWS == 0 and SEQ_TILE >= LEFT
    n = batch * seq
    assert n % ROW_TILE == 0
    d_ple = p.shape[-1]
    d_mix = D_ATT + D_GLA_V
    alpha = (2.0 * depth) ** 0.25

    x2 = x.reshape(n, d_model)
    p2 = p.reshape(n, d_ple)
    row = lambda v: v.reshape(1, -1).astype(_F32)
    w_main = w_in[0, :, :_OFF_LR].astype(_BF16)
    w_code = jnp.pad(w_in[0, :, _OFF_LR:], ((0, 0), (0, LANES - GLA_LOW_RANK))).astype(_BF16)
    wg_b = jnp.pad(w_gla_gate[0], ((0, LANES - GLA_LOW_RANK), (0, 0))).astype(_BF16)
    bias = _rel_bias_span(rel_bias[0])

    cparams = functools.partial(pltpu.CompilerParams, vmem_limit_bytes=VMEM_LIMIT)

    tiles_per_seq = seq // SEQ_TILE
    n_tiles = batch * tiles_per_seq
    tile_f32 = lambda width: pltpu.VMEM((SEQ_TILE, width), _F32)
    tile_bf16 = lambda width: pltpu.VMEM((SEQ_TILE, width), _BF16)
    projected = [tile_bf16(D_ATT), tile_f32(D_ATT), tile_f32(2 * D_GLA_K), tile_bf16(D_GLA_V),
                 tile_f32(D_GLA_V), tile_f32(D_GLA_K)]
    mix = pl.pallas_call(
        functools.partial(_proj_mix_kernel, seq_tile=SEQ_TILE, tiles_per_seq=tiles_per_seq),
        grid=(n_tiles + 1,),
        in_specs=[_const_spec((SEQ_TILE, d_model)),
                  pl.BlockSpec((SEQ_TILE, d_model), lambda k: (jnp.minimum(k + 1, n_tiles - 1), 0)),
                  _const_spec((1, d_model)), _const_spec((1, d_model)),
                  _const_spec((d_model, _OFF_LR)), _const_spec((d_model, LANES)),
                  _const_spec((LANES, D_GLA_K)), _const_spec((1, D_GLA_K)), _const_spec((1, D_GLA_V)),
                  _const_spec((N_HEAD_PAIRS, BIAS_TOP + BIAS_BOTTOM, 2 * PAIR_ROWS))],
        out_specs=pl.BlockSpec((SEQ_TILE, d_mix), lambda k: (jnp.maximum(k - 1, 0), 0)),
        out_shape=jax.ShapeDtypeStruct((n, d_mix), _BF16),
        scratch_shapes=[tile_bf16(d_model), tile_bf16(d_model),
                        tile_bf16(D_ATT), tile_bf16(D_ATT), tile_bf16(D_ATT),
                        tile_f32(D_ATT), tile_f32(2 * D_GLA_K), tile_bf16(D_GLA_V),
                        tile_f32(D_GLA_V), tile_f32(D_GLA_K)]
                       + projected
                       + [pltpu.VMEM((LEFT + SEQ_TILE, D_ATT), _BF16),
                          pltpu.VMEM(((LEFT + SEQ_TILE) // KEY_BLOCK, D_ATT, KEY_BLOCK), _BF16),
                          pltpu.VMEM((GLA_HEAD_V, D_GLA_K), _F32)],
        compiler_params=cparams(dimension_semantics=("arbitrary",)),
        name="proj_mixers",
    )(x2, x2, row(ln_in_g), row(ln_in_b), w_main, w_code, wg_b, row(b_gla_gate[0]), row(gla_norm_g[0]), bias)

    row_spec = lambda width: pl.BlockSpec((ROW_TILE, width), lambda i: (i, 0))
    out = pl.pallas_call(
        functools.partial(_epilogue_kernel, alpha=alpha),
        grid=(n // ROW_TILE,),
        in_specs=[row_spec(d_model), _const_spec((1, d_model)), _const_spec((1, d_model)),
                  row_spec(d_mix), row_spec(d_ple),
                  _const_spec((d_mix, d_model)), _const_spec((d_model, d_model)),
                  _const_spec((1, d_model)), _const_spec((d_ple, d_model)),
                  _const_spec((1, d_model)), _const_spec((1, d_model))],
        out_specs=row_spec(d_model),
        out_shape=jax.ShapeDtypeStruct((n, d_model), x.dtype),
        compiler_params=cparams(dimension_semantics=("parallel",)),
        name="out_proj_ple_norm",
    )(x2, row(ln_in_g), row(ln_in_b), mix, p2, w_out[0].astype(_BF16), w_ple_gate[0].astype(_BF16),
      row(b_ple_gate[0]), w_ple[0].astype(_BF16), row(ln_g[0]), row(ln_b[0]))
    return out.reshape(batch, seq, d_model)
```

```python
import functools
import math

import jax
import jax.numpy as jnp
from jax import lax
from jax.experimental import pallas as pl
from jax.experimental.pallas import tpu as pltpu

CHUNK = 64
N_ATT_HEADS = 8
ATT_HEAD_DIM = 64
D_ATT = N_ATT_HEADS * ATT_HEAD_DIM
LEFT_CHUNKS = 8
LEFT = LEFT_CHUNKS * CHUNK
BAND = LEFT + CHUNK
REL_CLIP = 128
N_GLA_HEADS = 4
GLA_HEAD_K = 64
GLA_HEAD_V = 128
D_GLA_K = N_GLA_HEADS * GLA_HEAD_K
D_GLA_V = N_GLA_HEADS * GLA_HEAD_V
GLA_LOW_RANK = 16
GLA_TAU = 16.0
LN_EPS = 1e-5
RMS_EPS = 1e-6
MASK_VALUE = -1e30
LOG2_E = math.log2(math.e)

LANES = 128
HEAD_PAIR = 2 * ATT_HEAD_DIM
assert HEAD_PAIR == LANES and 2 * GLA_HEAD_K == LANES
N_HEAD_PAIRS = N_ATT_HEADS // 2
PAIR_ROWS = 2 * CHUNK
KEY_SPAN = LEFT + PAIR_ROWS
KEY_BLOCK = LANES
N_SPAN_BLOCKS = KEY_SPAN // KEY_BLOCK
N_LEFT_BLOCKS = LEFT // KEY_BLOCK
assert PAIR_ROWS == KEY_BLOCK
BIAS_TOP = CHUNK
BIAS_BOTTOM = REL_CLIP + PAIR_ROWS
SUM_ROWS = 16

_OFF_AG = 3 * D_ATT
_OFF_GQK = 4 * D_ATT
_OFF_GV = _OFF_GQK + 2 * D_GLA_K
_OFF_GG = _OFF_GV + D_GLA_V
_OFF_LR = _OFF_GG + D_GLA_V
D_IN_PROJ = _OFF_LR + GLA_LOW_RANK
PROJ_PIECE = 256
N_MAIN_PIECES = _OFF_LR // PROJ_PIECE
N_PROJ_PIECES = N_MAIN_PIECES + 2
PROJ_SLOTS_PER_PAIR = 4

ROW_TILE = 1024
EPILOGUE_SPLIT = 4
SEQ_TILE = 512
VMEM_LIMIT = 56 * 1024 * 1024

_F32 = jnp.float32
_BF16 = jnp.bfloat16
_NT = (((1,), (1,)), ((), ()))
_TN = (((0,), (0,)), ((), ()))


def _layer_norm(xf, g, b):
    mu = jnp.mean(xf, axis=-1, keepdims=True)
    xc = xf - mu
    var = jnp.mean(xc * xc, axis=-1, keepdims=True)
    return xc * lax.rsqrt(var + LN_EPS) * g + b


def _silu(x):
    return x * jax.nn.sigmoid(x)


def _dot(a, b):
    return jnp.dot(a, b, preferred_element_type=_F32)


def _proj_mix_kernel(x0_ref, x_ref, g_ref, b_ref, w_ref, wc_ref, wg_ref, bg_ref, gn_ref, bias_ref,
                     mix_ref,
                     hb_ref, nhb_ref, code_ref, nq_ref, nk_ref, nv_ref, nag_ref, ngqk_ref, ngv_ref, ngg_ref, nla_ref,
                     q_ref, ag_ref, gqk_ref, gv_ref, gg_ref, loga_ref,
                     k_hist, vt_hist, state_ref, *, seq_tile, tiles_per_seq):
    k = pl.program_id(0)
    s = lax.rem(k + tiles_per_seq - 1, tiles_per_seq)
    n_new_blocks = seq_tile // KEY_BLOCK
    next_refs = (nq_ref, nk_ref, nv_ref, nag_ref, ngqk_ref, ngv_ref, ngg_ref, nla_ref)

    @pl.when(k == 0)
    def _():
        for ref in next_refs + (k_hist, vt_hist, state_ref):
            ref[...] = jnp.zeros_like(ref)
        nhb_ref[...] = _layer_norm(x0_ref[...], g_ref[...], b_ref[...]).astype(_BF16)

    @pl.when(s == 0)
    def _():
        k_hist[0:LEFT, :] = jnp.zeros((LEFT, D_ATT), _BF16)
        vt_hist[0:N_LEFT_BLOCKS] = jnp.zeros((N_LEFT_BLOCKS, D_ATT, KEY_BLOCK), _BF16)
        state_ref[...] = jnp.zeros_like(state_ref)

    @pl.when(s > 0)
    def _():
        k_hist[0:LEFT, :] = k_hist[seq_tile:seq_tile + LEFT, :]
        vt_hist[0:N_LEFT_BLOCKS] = vt_hist[n_new_blocks:n_new_blocks + N_LEFT_BLOCKS]

    k_hist[LEFT:LEFT + seq_tile, :] = nk_ref[...]
    for blk in range(n_new_blocks):
        vt_hist[N_LEFT_BLOCKS + blk] = nv_ref[blk * KEY_BLOCK:(blk + 1) * KEY_BLOCK, :].T
    for dst, src in ((q_ref, nq_ref), (ag_ref, nag_ref), (gqk_ref, ngqk_ref), (gv_ref, ngv_ref),
                     (gg_ref, ngg_ref), (loga_ref, nla_ref)):
        dst[...] = src[...]

    hb_ref[...] = nhb_ref[...]

    def norm_next(cp):
        slab = pl.ds(cp * PAIR_ROWS, PAIR_ROWS)
        nhb_ref[slab, :] = _layer_norm(x_ref[slab, :], g_ref[...], b_ref[...]).astype(_BF16)

    def proj_piece(i):
        if i == N_MAIN_PIECES:
            code_ref[...] = _dot(hb_ref[...], wc_ref[...]).astype(_BF16)
            return
        if i == N_MAIN_PIECES + 1:
            logit = _dot(code_ref[...], wg_ref[...]) + bg_ref[...]
            nla_ref[...] = jax.nn.log_sigmoid(logit) * (1.0 / GLA_TAU)
            return
        c0 = i * PROJ_PIECE
        acc = _dot(hb_ref[...], w_ref[:, c0:c0 + PROJ_PIECE])

        def put(ref, base, val):
            ref[:, c0 - base:c0 - base + PROJ_PIECE] = val.astype(ref.dtype)

        if c0 < D_ATT:
            put(nq_ref, 0, acc * (ATT_HEAD_DIM ** -0.5 * LOG2_E))
        elif c0 < 2 * D_ATT:
            put(nk_ref, D_ATT, acc)
        elif c0 < _OFF_AG:
            put(nv_ref, 2 * D_ATT, acc)
        elif c0 < _OFF_GQK:
            put(nag_ref, _OFF_AG, _silu(acc))
        elif c0 < _OFF_GQK + D_GLA_K:
            put(ngqk_ref, _OFF_GQK, acc * (GLA_HEAD_K ** -0.5))
        elif c0 < _OFF_GV:
            put(ngqk_ref, _OFF_GQK, acc)
        elif c0 < _OFF_GG:
            put(ngv_ref, _OFF_GV, acc)
        else:
            put(ngg_ref, _OFF_GG, _silu(acc) * gn_ref[:, c0 - _OFF_GG:c0 - _OFF_GG + PROJ_PIECE])

    lane = lax.broadcasted_iota(jnp.int32, (1, LANES), 1)
    low_half = lane < ATT_HEAD_DIM
    top_rows = lax.broadcasted_iota(jnp.int32, (LANES, 1), 0) < ATT_HEAD_DIM
    row_i = lax.broadcasted_iota(jnp.int32, (CHUNK, CHUNK), 0)
    col_i = lax.broadcasted_iota(jnp.int32, (CHUNK, CHUNK), 1)
    causal = row_i >= col_i
    chunk_row = lax.broadcasted_iota(jnp.int32, (CHUNK, 1), 0)
    ones_rows = jnp.ones((SUM_ROWS, KEY_SPAN), _BF16)

    def pair_body(cp, pieces):
        r0 = cp * PAIR_ROWS
        rows = pl.ds(r0, PAIR_ROWS)
        span = pl.ds(r0, KEY_SPAN)
        first_key = s * seq_tile + cp * PAIR_ROWS - LEFT
        pieces = list(pieces)

        def proj_next():
            piece = pieces.pop(0)
            if piece is not None:
                proj_piece(piece)

        def att_scores(j):
            cols = slice(j * LANES, (j + 1) * LANES)
            qp = q_ref[rows, cols]
            zero = jnp.zeros_like(qp)
            q_blk = jnp.concatenate([jnp.where(low_half, qp, zero), jnp.where(low_half, zero, qp)], axis=0)
            kp = k_hist[span, cols]
            sc = lax.dot_general(kp, q_blk, _NT, preferred_element_type=_F32)
            slabs = [sc[0:BIAS_TOP] + bias_ref[j, 0:BIAS_TOP, :],
                     sc[BIAS_TOP:KEY_SPAN - BIAS_BOTTOM],
                     sc[KEY_SPAN - BIAS_BOTTOM:] + bias_ref[j, BIAS_TOP:, :]]
            sc = jnp.concatenate(slabs, axis=0)
            blocks = [jnp.where(first_key + i * KEY_BLOCK >= 0, sc[i * KEY_BLOCK:(i + 1) * KEY_BLOCK, :], MASK_VALUE)
                      for i in range(N_SPAN_BLOCKS)]
            return jnp.concatenate(blocks, axis=0)

        def att_softmax(sc):
            m = jnp.max(sc, axis=0, keepdims=True)
            return jnp.exp2(sc - m).astype(_BF16)

        def att_out(j, pe):
            cols = slice(j * LANES, (j + 1) * LANES)
            vt = jnp.concatenate([vt_hist[cp + i, cols, :] for i in range(N_SPAN_BLOCKS)], axis=1)
            ot = _dot(jnp.concatenate([vt, ones_rows], axis=0), pe)
            inv = 1.0 / ot[LANES:LANES + 1, :]
            num = jnp.where(top_rows, ot[0:LANES, 0:LANES], ot[0:LANES, LANES:2 * LANES])
            scale = jnp.where(top_rows, inv[:, 0:LANES], inv[:, LANES:2 * LANES])
            att = (num * scale).T
            mix_ref[rows, cols] = (att * ag_ref[rows, cols]).astype(_BF16)

        def gla_cumsum(c):
            crow = pl.ds(cp * PAIR_ROWS + c * CHUNK, CHUNK)
            cum = loga_ref[crow, :]
            shift = 1
            while shift < CHUNK:
                cum = cum + jnp.where(chunk_row >= shift, pltpu.roll(cum, shift, axis=0), 0.0)
                shift *= 2
            return cum

        def gla_scores(c, cum):
            crow = pl.ds(cp * PAIR_ROWS + c * CHUNK, CHUNK)
            cum_end = cum[CHUNK - 1:CHUNK, :]
            e_pos = jnp.exp(cum)
            e_neg = jnp.exp(-cum)
            gq = gqk_ref[crow, 0:D_GLA_K]
            gk = gqk_ref[crow, D_GLA_K:2 * D_GLA_K]
            q_fwd = gq * e_pos
            q_bwd = gq * e_neg
            k_fwd = (gk * e_pos).astype(_BF16)
            k_bwd = (gk * e_neg).astype(_BF16)
            k_end = (gk * jnp.exp(cum_end - cum)).astype(_BF16)
            decay = jnp.exp(cum_end)
            parts = []
            for g in range(N_GLA_HEADS // 2):
                pcols = slice(g * LANES, (g + 1) * LANES)
                prows = slice(g * LANES, (g + 1) * LANES)
                state = state_ref[prows, :]
                state_b = state.astype(_BF16)
                for e in range(2):
                    sel = low_half if e == 0 else jnp.logical_not(low_half)
                    qf = jnp.where(sel, q_fwd[:, pcols], 0.0).astype(_BF16)
                    qb = jnp.where(sel, q_bwd[:, pcols], 0.0).astype(_BF16)
                    a_causal = lax.dot_general(qf, k_bwd[:, pcols], _NT, preferred_element_type=_F32)
                    a_anti = lax.dot_general(qb, k_fwd[:, pcols], _NT, preferred_element_type=_F32)
                    parts.append((a_causal, a_anti, _dot(qf, state_b)))
                v_pair = gv_ref[crow, 2 * g * GLA_HEAD_V:(2 * g + 2) * GLA_HEAD_V]
                kv = lax.dot_general(k_end[:, pcols], v_pair, _TN, preferred_element_type=_F32)
                new_rows = jnp.where(top_rows, kv[:, 0:GLA_HEAD_V], kv[:, GLA_HEAD_V:])
                decay_rows = jnp.broadcast_to(decay[:, pcols], (LANES, LANES)).T
                state_ref[prows, :] = decay_rows * state + new_rows
            return parts

        def gla_out(c, parts):
            crow = pl.ds(cp * PAIR_ROWS + c * CHUNK, CHUNK)
            for h in range(N_GLA_HEADS):
                a_causal, a_anti, o_inter = parts[h]
                hv = slice(h * GLA_HEAD_V, (h + 1) * GLA_HEAD_V)
                att = jnp.where(causal, a_causal, a_anti).astype(_BF16)
                o = _dot(att, gv_ref[crow, hv]) + o_inter
                o = o * lax.rsqrt(jnp.mean(o * o, axis=-1, keepdims=True) + RMS_EPS)
                mix_ref[crow, D_ATT + h * GLA_HEAD_V:D_ATT + (h + 1) * GLA_HEAD_V] = (
                    o * gg_ref[crow, hv]).astype(_BF16)

        cum0 = gla_cumsum(0)
        cum1 = gla_cumsum(1)
        sc0 = att_scores(0)
        sc1 = att_scores(1)
        g0 = gla_scores(0, cum0)
        proj_next()
        att_out(0, att_softmax(sc0))
        sc2 = att_scores(2)
        gla_out(0, g0)
        g1 = gla_scores(1, cum1)
        proj_next()
        att_out(1, att_softmax(sc1))
        sc3 = att_scores(3)
        gla_out(1, g1)
        norm_next(cp)
        proj_next()
        att_out(2, att_softmax(sc2))
        proj_next()
        att_out(3, att_softmax(sc3))
        assert not pieces

    n_pairs = seq_tile // PAIR_ROWS
    slots = PROJ_SLOTS_PER_PAIR * n_pairs
    heavy = [c // PROJ_PIECE for c in range(_OFF_AG, _OFF_GQK, PROJ_PIECE)] \
        + [c // PROJ_PIECE for c in range(_OFF_GG, _OFF_LR, PROJ_PIECE)] + [N_MAIN_PIECES + 1]
    light = [N_MAIN_PIECES] + [i for i in range(N_MAIN_PIECES) if i not in heavy]
    schedule = []
    for slot in range(slots):
        first = heavy if slot % PROJ_SLOTS_PER_PAIR == 0 else light
        source = first or heavy or light
        schedule.append(source.pop(0) if source else None)
    assert not heavy and not light
    for cp in range(n_pairs):
        pair_body(cp, schedule[cp * PROJ_SLOTS_PER_PAIR:(cp + 1) * PROJ_SLOTS_PER_PAIR])


def _epilogue_kernel(x_ref, g_ref, b_ref, mix_ref, p_ref, wo_ref, wpg_ref, bpg_ref, wp_ref,
                     og_ref, ob_ref, out_ref, *, alpha):
    half = x_ref.shape[0] // EPILOGUE_SPLIT
    halves = [pl.ds(i * half, half) for i in range(EPILOGUE_SPLIT)]

    def residual(rows):
        h = _layer_norm(x_ref[rows, :], g_ref[...], b_ref[...])
        return alpha * h + _dot(mix_ref[rows, :], wo_ref[...])

    def gate_and_embed(rows, r):
        logit = _dot(r.astype(_BF16), wpg_ref[...])
        return logit, _dot(p_ref[rows, :].astype(_BF16), wp_ref[...])

    def finish(rows, r, logit, ple):
        r = r + jax.nn.sigmoid(logit + bpg_ref[...]) * ple
        out_ref[rows, :] = _layer_norm(r, og_ref[...], ob_ref[...])

    rs = [residual(rows) for rows in halves]
    ts = [gate_and_embed(rows, r) for rows, r in zip(halves, rs)]
    for rows, r, t in zip(halves, rs, ts):
        finish(rows, r, *t)


def _const_spec(shape):
    zeros = (0,) * len(shape)
    return pl.BlockSpec(shape, lambda *_: zeros, pipeline_mode=pl.Buffered(1))


def _rel_bias_span(rel_table):
    table = rel_table.astype(_F32)
    heads = table.shape[0]
    near = jnp.flip(table[:, REL_CLIP - (CHUNK - 1):2 * REL_CLIP], axis=1) - table[:, 2 * REL_CLIP:]
    ext = jnp.concatenate([jnp.zeros((heads, BAND - REL_CLIP), _F32), near], axis=1) * LOG2_E
    width = ext.shape[1] + 1
    tiled = jnp.tile(jnp.pad(ext, ((0, 0), (0, 1))), (1, CHUNK))[:, :CHUNK * (width - 1)]
    band = tiled.reshape(heads, CHUNK, width - 1)[:, :, CHUNK - 1:CHUNK - 1 + BAND]
    pad = lambda lo, hi: jnp.pad(band, ((0, 0), (0, 0), (lo, hi)), constant_values=MASK_VALUE)
    both = jnp.stack([pad(0, CHUNK), pad(CHUNK, 0)], axis=1)
    both = both.reshape(N_HEAD_PAIRS, 2, 2, CHUNK, KEY_SPAN)
    full = both.transpose(0, 4, 1, 2, 3).reshape(N_HEAD_PAIRS, KEY_SPAN, 2 * PAIR_ROWS)
    return jnp.concatenate([full[:, :BIAS_TOP], full[:, KEY_SPAN - BIAS_BOTTOM:]], axis=1)


def kernel(x, p, ln_in_g, ln_in_b, w_in, w_gla_gate, b_gla_gate, rel_bias,
           gla_norm_g, w_out, w_ple, w_ple_gate, b_ple_gate, ln_g, ln_b):
    batch, seq, d_model = x.shape
    depth = w_in.shape[0]
    assert depth == 1, "single-layer stack only"
    assert w_in.shape[2] == D_IN_PROJ
    assert seq % SEQ_TILE == 0 and SEQ_TILE % PAIR_ROWS == 0 and SEQ_TILE >= LEFT
    n = batch * seq
    assert n % ROW_TILE == 0
    d_ple = p.shape[-1]
    d_mix = D_ATT + D_GLA_V
    alpha = (2.0 * depth) ** 0.25

    x2 = x.reshape(n, d_model)
    p2 = p.reshape(n, d_ple)
    row = lambda v: v.reshape(1, -1).astype(_F32)
    w_main = w_in[0, :, :_OFF_LR].astype(_BF16)
    w_code = jnp.pad(w_in[0, :, _OFF_LR:], ((0, 0), (0, LANES - GLA_LOW_RANK))).astype(_BF16)
    wg_b = jnp.pad(w_gla_gate[0], ((0, LANES - GLA_LOW_RANK), (0, 0))).astype(_BF16)
    bias = _rel_bias_span(rel_bias[0])

    cparams = functools.partial(pltpu.CompilerParams, vmem_limit_bytes=VMEM_LIMIT)

    tiles_per_seq = seq // SEQ_TILE
    n_tiles = batch * tiles_per_seq
    tile_f32 = lambda width: pltpu.VMEM((SEQ_TILE, width), _F32)
    tile_bf16 = lambda width: pltpu.VMEM((SEQ_TILE, width), _BF16)
    projected = [tile_bf16(D_ATT), tile_f32(D_ATT), tile_f32(2 * D_GLA_K), tile_bf16(D_GLA_V),
                 tile_f32(D_GLA_V), tile_f32(D_GLA_K)]
    mix = pl.pallas_call(
        functools.partial(_proj_mix_kernel, seq_tile=SEQ_TILE, tiles_per_seq=tiles_per_seq),
        grid=(n_tiles + 1,),
        in_specs=[_const_spec((SEQ_TILE, d_model)),
                  pl.BlockSpec((SEQ_TILE, d_model), lambda k: (jnp.minimum(k + 1, n_tiles - 1), 0)),
                  _const_spec((1, d_model)), _const_spec((1, d_model)),
                  _const_spec((d_model, _OFF_LR)), _const_spec((d_model, LANES)),
                  _const_spec((LANES, D_GLA_K)), _const_spec((1, D_GLA_K)), _const_spec((1, D_GLA_V)),
                  _const_spec((N_HEAD_PAIRS, BIAS_TOP + BIAS_BOTTOM, 2 * PAIR_ROWS))],
        out_specs=pl.BlockSpec((SEQ_TILE, d_mix), lambda k: (jnp.maximum(k - 1, 0), 0)),
        out_shape=jax.ShapeDtypeStruct((n, d_mix), _BF16),
        scratch_shapes=[tile_bf16(d_model), tile_bf16(d_model),
                        tile_bf16(LANES),
                        tile_bf16(D_ATT), tile_bf16(D_ATT), tile_bf16(D_ATT),
                        tile_f32(D_ATT), tile_f32(2 * D_GLA_K), tile_bf16(D_GLA_V),
                        tile_f32(D_GLA_V), tile_f32(D_GLA_K)]
                       + projected
                       + [pltpu.VMEM((LEFT + SEQ_TILE, D_ATT), _BF16),
                          pltpu.VMEM(((LEFT + SEQ_TILE) // KEY_BLOCK, D_ATT, KEY_BLOCK), _BF16),
                          pltpu.VMEM((D_GLA_K, GLA_HEAD_V), _F32)],
        compiler_params=cparams(dimension_semantics=("arbitrary",)),
        name="proj_mixers",
    )(x2, x2, row(ln_in_g), row(ln_in_b), w_main, w_code, wg_b, row(b_gla_gate[0]), row(gla_norm_g[0]), bias)

    row_spec = lambda width: pl.BlockSpec((ROW_TILE, width), lambda i: (i, 0))
    out = pl.pallas_call(
        functools.partial(_epilogue_kernel, alpha=alpha),
        grid=(n // ROW_TILE,),
        in_specs=[row_spec(d_model), _const_spec((1, d_model)), _const_spec((1, d_model)),
                  row_spec(d_mix), row_spec(d_ple),
                  _const_spec((d_mix, d_model)), _const_spec((d_model, d_model)),
                  _const_spec((1, d_model)), _const_spec((d_ple, d_model)),
                  _const_spec((1, d_model)), _const_spec((1, d_model))],
        out_specs=row_spec(d_model),
        out_shape=jax.ShapeDtypeStruct((n, d_model), x.dtype),
        compiler_params=cparams(dimension_semantics=("parallel",)),
        name="out_proj_ple_norm",
    )(x2, row(ln_in_g), row(ln_in_b), mix, p2, w_out[0].astype(_BF16), w_ple_gate[0].astype(_BF16),
      row(b_ple_gate[0]), w_ple[0].astype(_BF16), row(ln_g[0]), row(ln_b[0]))
    return out.reshape(batch, seq, d_model)
```

```python
import functools
import math

import jax
import jax.numpy as jnp
from jax import lax
from jax.experimental import pallas as pl
from jax.experimental.pallas import tpu as pltpu

CHUNK = 64
N_ATT_HEADS = 8
ATT_HEAD_DIM = 64
D_ATT = N_ATT_HEADS * ATT_HEAD_DIM
LEFT_CHUNKS = 8
LEFT = LEFT_CHUNKS * CHUNK
BAND = LEFT + CHUNK
REL_CLIP = 128
N_GLA_HEADS = 4
GLA_HEAD_K = 64
GLA_HEAD_V = 128
D_GLA_K = N_GLA_HEADS * GLA_HEAD_K
D_GLA_V = N_GLA_HEADS * GLA_HEAD_V
GLA_LOW_RANK = 16
GLA_TAU = 16.0
LN_EPS = 1e-5
RMS_EPS = 1e-6
MASK_VALUE = -1e30
LOG2_E = math.log2(math.e)

LANES = 128
HEAD_PAIR = 2 * ATT_HEAD_DIM
assert HEAD_PAIR == LANES and 2 * GLA_HEAD_K == LANES
N_HEAD_PAIRS = N_ATT_HEADS // 2
PAIR_ROWS = 2 * CHUNK
KEY_SPAN = LEFT + PAIR_ROWS
KEY_BLOCK = LANES
N_SPAN_BLOCKS = KEY_SPAN // KEY_BLOCK
N_LEFT_BLOCKS = LEFT // KEY_BLOCK
assert PAIR_ROWS == KEY_BLOCK
BIAS_TOP = CHUNK
BIAS_BOTTOM = REL_CLIP + PAIR_ROWS
SUM_ROWS = 16
PV_KEY_TILE = 256
N_PV_TILES = -(-KEY_SPAN // PV_KEY_TILE)
assert N_PV_TILES == 3

_OFF_AG = 3 * D_ATT
_OFF_GQK = 4 * D_ATT
_OFF_GV = _OFF_GQK + 2 * D_GLA_K
_OFF_GG = _OFF_GV + D_GLA_V
_OFF_LR = _OFF_GG + D_GLA_V
D_IN_PROJ = _OFF_LR + GLA_LOW_RANK
PROJ_PIECE = 256
N_MAIN_PIECES = _OFF_LR // PROJ_PIECE
N_PROJ_PIECES = N_MAIN_PIECES + 2
PROJ_SLOTS_PER_PAIR = 4

ROW_TILE = 1024
EPILOGUE_SPLIT = 4
SEQ_TILE = 512
VMEM_LIMIT = 56 * 1024 * 1024

_F32 = jnp.float32
_BF16 = jnp.bfloat16
_NT = (((1,), (1,)), ((), ()))
_TN = (((0,), (0,)), ((), ()))


def _layer_norm(xf, g, b):
    mu = jnp.mean(xf, axis=-1, keepdims=True)
    xc = xf - mu
    var = jnp.mean(xc * xc, axis=-1, keepdims=True)
    return xc * lax.rsqrt(var + LN_EPS) * g + b


def _silu(x):
    return x * jax.nn.sigmoid(x)


def _dot(a, b):
    return jnp.dot(a, b, preferred_element_type=_F32)


def _proj_mix_kernel(x0_ref, x_ref, g_ref, b_ref, w32_ref, wc_ref, wg_ref, bg_ref, gn_ref, bias_ref,
                     mix_ref,
                     w_ref, hb_ref, nhb_ref, code_ref, nq_ref, nk_ref, nv_ref, nag_ref, ngqk_ref, ngv_ref, ngg_ref, nla_ref,
                     q_ref, ag_ref, gqk_ref, gv_ref, gg_ref, loga_ref,
                     k_hist, vt_hist, state_ref, *, seq_tile, tiles_per_seq):
    k = pl.program_id(0)
    s = lax.rem(k + tiles_per_seq - 1, tiles_per_seq)
    n_new_blocks = seq_tile // KEY_BLOCK
    next_refs = (nq_ref, nk_ref, nv_ref, nag_ref, ngqk_ref, ngv_ref, ngg_ref, nla_ref)

    @pl.when(k == 0)
    def _():
        for ref in next_refs + (k_hist, vt_hist, state_ref):
            ref[...] = jnp.zeros_like(ref)
        nhb_ref[...] = _layer_norm(x0_ref[...], g_ref[...], b_ref[...]).astype(_BF16)
        for c0 in range(0, _OFF_LR, PROJ_PIECE):
            w_ref[:, c0:c0 + PROJ_PIECE] = w32_ref[:, c0:c0 + PROJ_PIECE].astype(_BF16)

    @pl.when(s == 0)
    def _():
        k_hist[0:LEFT, :] = jnp.zeros((LEFT, D_ATT), _BF16)
        vt_hist[0:N_LEFT_BLOCKS] = jnp.zeros((N_LEFT_BLOCKS, D_ATT, KEY_BLOCK), _BF16)
        state_ref[...] = jnp.zeros_like(state_ref)

    @pl.when(s > 0)
    def _():
        k_hist[0:LEFT, :] = k_hist[seq_tile:seq_tile + LEFT, :]
        vt_hist[0:N_LEFT_BLOCKS] = vt_hist[n_new_blocks:n_new_blocks + N_LEFT_BLOCKS]

    k_hist[LEFT:LEFT + seq_tile, :] = nk_ref[...]
    for blk in range(n_new_blocks):
        vt_hist[N_LEFT_BLOCKS + blk] = nv_ref[blk * KEY_BLOCK:(blk + 1) * KEY_BLOCK, :].T
    for dst, src in ((q_ref, nq_ref), (ag_ref, nag_ref), (gqk_ref, ngqk_ref), (gv_ref, ngv_ref),
                     (gg_ref, ngg_ref), (loga_ref, nla_ref)):
        dst[...] = src[...]

    hb_ref[...] = nhb_ref[...]

    def norm_next(cp):
        slab = pl.ds(cp * PAIR_ROWS, PAIR_ROWS)
        nhb_ref[slab, :] = _layer_norm(x_ref[slab, :], g_ref[...], b_ref[...]).astype(_BF16)

    def proj_piece(i):
        if i == N_MAIN_PIECES:
            code_ref[...] = _dot(hb_ref[...], wc_ref[...]).astype(_BF16)
            return
        if i == N_MAIN_PIECES + 1:
            logit = _dot(code_ref[...], wg_ref[...]) + bg_ref[...]
            nla_ref[...] = jax.nn.log_sigmoid(logit) * (1.0 / GLA_TAU)
            return
        c0 = i * PROJ_PIECE
        acc = _dot(hb_ref[...], w_ref[:, c0:c0 + PROJ_PIECE])

        def put(ref, base, val):
            ref[:, c0 - base:c0 - base + PROJ_PIECE] = val.astype(ref.dtype)

        if c0 < D_ATT:
            put(nq_ref, 0, acc * (ATT_HEAD_DIM ** -0.5 * LOG2_E))
        elif c0 < 2 * D_ATT:
            put(nk_ref, D_ATT, acc)
        elif c0 < _OFF_AG:
            put(nv_ref, 2 * D_ATT, acc)
        elif c0 < _OFF_GQK:
            put(nag_ref, _OFF_AG, _silu(acc))
        elif c0 < _OFF_GQK + D_GLA_K:
            put(ngqk_ref, _OFF_GQK, acc * (GLA_HEAD_K ** -0.5))
        elif c0 < _OFF_GV:
            put(ngqk_ref, _OFF_GQK, acc)
        elif c0 < _OFF_GG:
            put(ngv_ref, _OFF_GV, acc)
        else:
            put(ngg_ref, _OFF_GG, _silu(acc) * gn_ref[:, c0 - _OFF_GG:c0 - _OFF_GG + PROJ_PIECE])

    lane = lax.broadcasted_iota(jnp.int32, (1, LANES), 1)
    low_half = lane < ATT_HEAD_DIM
    top_rows = lax.broadcasted_iota(jnp.int32, (LANES, 1), 0) < ATT_HEAD_DIM
    row_i = lax.broadcasted_iota(jnp.int32, (CHUNK, CHUNK), 0)
    col_i = lax.broadcasted_iota(jnp.int32, (CHUNK, CHUNK), 1)
    causal = row_i >= col_i
    chunk_row = lax.broadcasted_iota(jnp.int32, (CHUNK, 1), 0)
    ones_rows = jnp.ones((SUM_ROWS, KEY_SPAN), _BF16)

    def pair_body(cp, pieces):
        r0 = cp * PAIR_ROWS
        rows = pl.ds(r0, PAIR_ROWS)
        span = pl.ds(r0, KEY_SPAN)
        first_key = s * seq_tile + cp * PAIR_ROWS - LEFT
        pieces = list(pieces)

        def proj_next():
            piece = pieces.pop(0)
            if piece is not None:
                proj_piece(piece)

        def att_scores(j):
            cols = slice(j * LANES, (j + 1) * LANES)
            qp = q_ref[rows, cols]
            zero = jnp.zeros_like(qp)
            q_blk = jnp.concatenate([jnp.where(low_half, qp, zero), jnp.where(low_half, zero, qp)], axis=0)
            kp = k_hist[span, cols]
            sc = lax.dot_general(kp, q_blk, _NT, preferred_element_type=_F32)
            slabs = [sc[0:BIAS_TOP] + bias_ref[j, 0:BIAS_TOP, :],
                     sc[BIAS_TOP:KEY_SPAN - BIAS_BOTTOM],
                     sc[KEY_SPAN - BIAS_BOTTOM:] + bias_ref[j, BIAS_TOP:, :]]
            sc = jnp.concatenate(slabs, axis=0)
            blocks = [jnp.where(first_key + i * KEY_BLOCK >= 0, sc[i * KEY_BLOCK:(i + 1) * KEY_BLOCK, :], MASK_VALUE)
                      for i in range(N_SPAN_BLOCKS)]
            return jnp.concatenate(blocks, axis=0)

        def att_softmax(sc):
            m = jnp.max(sc, axis=0, keepdims=True)
            return jnp.exp2(sc - m).astype(_BF16)

        def att_out(j, pe):
            cols = slice(j * LANES, (j + 1) * LANES)
            partial = []

            def key_tile(t):
                k0, k1 = t * PV_KEY_TILE, min((t + 1) * PV_KEY_TILE, KEY_SPAN)
                vt = jnp.concatenate([vt_hist[cp + i, cols, :] for i in range(k0 // KEY_BLOCK, k1 // KEY_BLOCK)],
                                     axis=1)
                partial.append(_dot(jnp.concatenate([vt, ones_rows[:, k0:k1]], axis=0), pe[k0:k1, :]))

            def finish():
                ot = functools.reduce(jnp.add, partial)
                inv = 1.0 / ot[LANES:LANES + 1, :]
                num = jnp.where(top_rows, ot[0:LANES, 0:LANES], ot[0:LANES, LANES:2 * LANES])
                scale = jnp.where(top_rows, inv[:, 0:LANES], inv[:, LANES:2 * LANES])
                att = (num * scale).T
                mix_ref[rows, cols] = (att * ag_ref[rows, cols]).astype(_BF16)

            return key_tile, finish

        def gla_cumsum(c):
            crow = pl.ds(cp * PAIR_ROWS + c * CHUNK, CHUNK)
            cum = loga_ref[crow, :]
            shift = 1
            while shift < CHUNK:
                cum = cum + jnp.where(chunk_row >= shift, pltpu.roll(cum, shift, axis=0), 0.0)
                shift *= 2
            return cum

        def gla_scores(c, cum):
            crow = pl.ds(cp * PAIR_ROWS + c * CHUNK, CHUNK)
            cum_end = cum[CHUNK - 1:CHUNK, :]
            e_pos = jnp.exp(cum)
            e_neg = jnp.exp(-cum)
            gq = gqk_ref[crow, 0:D_GLA_K]
            gk = gqk_ref[crow, D_GLA_K:2 * D_GLA_K]
            q_fwd = gq * e_pos
            q_bwd = gq * e_neg
            k_fwd = (gk * e_pos).astype(_BF16)
            k_bwd = (gk * e_neg).astype(_BF16)
            k_end = (gk * jnp.exp(cum_end - cum)).astype(_BF16)
            decay = jnp.exp(cum_end)
            parts = []
            for g in range(N_GLA_HEADS // 2):
                pcols = slice(g * LANES, (g + 1) * LANES)
                prows = slice(g * LANES, (g + 1) * LANES)
                state = state_ref[prows, :]
                state_b = state.astype(_BF16)
                for e in range(2):
                    sel = low_half if e == 0 else jnp.logical_not(low_half)
                    qf = jnp.where(sel, q_fwd[:, pcols], 0.0).astype(_BF16)
                    qb = jnp.where(sel, q_bwd[:, pcols], 0.0).astype(_BF16)
                    a_causal = lax.dot_general(qf, k_bwd[:, pcols], _NT, preferred_element_type=_F32)
                    a_anti = lax.dot_general(qb, k_fwd[:, pcols], _NT, preferred_element_type=_F32)
                    parts.append((a_causal, a_anti, _dot(qf, state_b)))
                v_pair = gv_ref[crow, 2 * g * GLA_HEAD_V:(2 * g + 2) * GLA_HEAD_V]
                kv = lax.dot_general(k_end[:, pcols], v_pair, _TN, preferred_element_type=_F32)
                new_rows = jnp.where(top_rows, kv[:, 0:GLA_HEAD_V], kv[:, GLA_HEAD_V:])
                decay_rows = jnp.broadcast_to(decay[:, pcols], (LANES, LANES)).T
                state_ref[prows, :] = decay_rows * state + new_rows
            return parts

        def gla_out(c, parts):
            crow = pl.ds(cp * PAIR_ROWS + c * CHUNK, CHUNK)
            for h in range(N_GLA_HEADS):
                a_causal, a_anti, o_inter = parts[h]
                hv = slice(h * GLA_HEAD_V, (h + 1) * GLA_HEAD_V)
                att = jnp.where(causal, a_causal, a_anti).astype(_BF16)
                o = _dot(att, gv_ref[crow, hv]) + o_inter
                o = o * lax.rsqrt(jnp.mean(o * o, axis=-1, keepdims=True) + RMS_EPS)
                mix_ref[crow, D_ATT + h * GLA_HEAD_V:D_ATT + (h + 1) * GLA_HEAD_V] = (
                    o * gg_ref[crow, hv]).astype(_BF16)

        cum0 = gla_cumsum(0)
        cum1 = gla_cumsum(1)
        sc0 = att_scores(0)
        sc1 = att_scores(1)
        g0 = gla_scores(0, cum0)
        pv, finish = att_out(0, att_softmax(sc0))
        pv(0)
        proj_next()
        pv(1)
        sc2 = att_scores(2)
        pv(2)
        finish()
        gla_out(0, g0)
        g1 = gla_scores(1, cum1)
        pv, finish = att_out(1, att_softmax(sc1))
        pv(0)
        proj_next()
        pv(1)
        sc3 = att_scores(3)
        pv(2)
        finish()
        gla_out(1, g1)
        norm_next(cp)
        for j in (2, 3):
            pv, finish = att_out(j, att_softmax(sc2 if j == 2 else sc3))
            pv(0)
            proj_next()
            pv(1)
            pv(2)
            finish()
        assert not pieces

    n_pairs = seq_tile // PAIR_ROWS
    slots = PROJ_SLOTS_PER_PAIR * n_pairs
    heavy = [c // PROJ_PIECE for c in range(_OFF_AG, _OFF_GQK, PROJ_PIECE)] \
        + [c // PROJ_PIECE for c in range(_OFF_GG, _OFF_LR, PROJ_PIECE)] + [N_MAIN_PIECES + 1]
    light = [N_MAIN_PIECES] + [i for i in range(N_MAIN_PIECES) if i not in heavy]
    schedule = []
    for slot in range(slots):
        first = heavy if slot % PROJ_SLOTS_PER_PAIR == 0 else light
        source = first or heavy or light
        schedule.append(source.pop(0) if source else None)
    assert not heavy and not light
    for cp in range(n_pairs):
        pair_body(cp, schedule[cp * PROJ_SLOTS_PER_PAIR:(cp + 1) * PROJ_SLOTS_PER_PAIR])


def _epilogue_kernel(x_ref, g_ref, b_ref, mix_ref, p_ref, wo_ref, wpg_ref, bpg_ref, wp_ref,
                     og_ref, ob_ref, out_ref, *, alpha):
    half = x_ref.shape[0] // EPILOGUE_SPLIT
    halves = [pl.ds(i * half, half) for i in range(EPILOGUE_SPLIT)]

    def residual(rows):
        h = _layer_norm(x_ref[rows, :], g_ref[...], b_ref[...])
        return alpha * h + _dot(mix_ref[rows, :], wo_ref[...])

    def gate_and_embed(rows, r):
        logit = _dot(r.astype(_BF16), wpg_ref[...])
        return logit, _dot(p_ref[rows, :].astype(_BF16), wp_ref[...])

    def finish(rows, r, logit, ple):
        r = r + jax.nn.sigmoid(logit + bpg_ref[...]) * ple
        out_ref[rows, :] = _layer_norm(r, og_ref[...], ob_ref[...])

    rs = [residual(rows) for rows in halves]
    ts = [gate_and_embed(rows, r) for rows, r in zip(halves, rs)]
    for rows, r, t in zip(halves, rs, ts):
        finish(rows, r, *t)


def _const_spec(shape):
    zeros = (0,) * len(shape)
    return pl.BlockSpec(shape, lambda *_: zeros, pipeline_mode=pl.Buffered(1))


def _rel_bias_span(rel_table):
    table = rel_table.astype(_F32)
    heads = table.shape[0]
    near = jnp.flip(table[:, REL_CLIP - (CHUNK - 1):2 * REL_CLIP], axis=1) - table[:, 2 * REL_CLIP:]
    ext = jnp.concatenate([jnp.zeros((heads, BAND - REL_CLIP), _F32), near], axis=1) * LOG2_E
    width = ext.shape[1] + 1
    tiled = jnp.tile(jnp.pad(ext, ((0, 0), (0, 1))), (1, CHUNK))[:, :CHUNK * (width - 1)]
    band = tiled.reshape(heads, CHUNK, width - 1)[:, :, CHUNK - 1:CHUNK - 1 + BAND]
    pad = lambda lo, hi: jnp.pad(band, ((0, 0), (0, 0), (lo, hi)), constant_values=MASK_VALUE)
    both = jnp.stack([pad(0, CHUNK), pad(CHUNK, 0)], axis=1)
    both = both.reshape(N_HEAD_PAIRS, 2, 2, CHUNK, KEY_SPAN)
    full = both.transpose(0, 4, 1, 2, 3).reshape(N_HEAD_PAIRS, KEY_SPAN, 2 * PAIR_ROWS)
    return jnp.concatenate([full[:, :BIAS_TOP], full[:, KEY_SPAN - BIAS_BOTTOM:]], axis=1)


def kernel(x, p, ln_in_g, ln_in_b, w_in, w_gla_gate, b_gla_gate, rel_bias,
           gla_norm_g, w_out, w_ple, w_ple_gate, b_ple_gate, ln_g, ln_b):
    batch, seq, d_model = x.shape
    depth = w_in.shape[0]
    assert depth == 1, "single-layer stack only"
    assert w_in.shape[2] == D_IN_PROJ
    assert seq % SEQ_TILE == 0 and SEQ_TILE % PAIR_ROWS == 0 and SEQ_TILE >= LEFT
    n = batch * seq
    assert n % ROW_TILE == 0
    d_ple = p.shape[-1]
    d_mix = D_ATT + D_GLA_V
    alpha = (2.0 * depth) ** 0.25

    x2 = x.reshape(n, d_model)
    p2 = p.reshape(n, d_ple)
    row = lambda v: v.reshape(1, -1).astype(_F32)
    w_code = jnp.pad(w_in[0, :, _OFF_LR:], ((0, 0), (0, LANES - GLA_LOW_RANK))).astype(_BF16)
    wg_b = jnp.pad(w_gla_gate[0], ((0, LANES - GLA_LOW_RANK), (0, 0))).astype(_BF16)
    bias = _rel_bias_span(rel_bias[0])

    cparams = functools.partial(pltpu.CompilerParams, vmem_limit_bytes=VMEM_LIMIT)

    tiles_per_seq = seq // SEQ_TILE
    n_tiles = batch * tiles_per_seq
    tile_f32 = lambda width: pltpu.VMEM((SEQ_TILE, width), _F32)
    tile_bf16 = lambda width: pltpu.VMEM((SEQ_TILE, width), _BF16)
    projected = [tile_bf16(D_ATT), tile_f32(D_ATT), tile_f32(2 * D_GLA_K), tile_bf16(D_GLA_V),
                 tile_f32(D_GLA_V), tile_f32(D_GLA_K)]
    mix = pl.pallas_call(
        functools.partial(_proj_mix_kernel, seq_tile=SEQ_TILE, tiles_per_seq=tiles_per_seq),
        grid=(n_tiles + 1,),
        in_specs=[_const_spec((SEQ_TILE, d_model)),
                  pl.BlockSpec((SEQ_TILE, d_model), lambda k: (jnp.minimum(k + 1, n_tiles - 1), 0)),
                  _const_spec((1, d_model)), _const_spec((1, d_model)),
                  pl.BlockSpec((None, d_model, D_IN_PROJ), lambda k: (0, 0, 0), pipeline_mode=pl.Buffered(1)),
                  _const_spec((d_model, LANES)),
                  _const_spec((LANES, D_GLA_K)), _const_spec((1, D_GLA_K)), _const_spec((1, D_GLA_V)),
                  _const_spec((N_HEAD_PAIRS, BIAS_TOP + BIAS_BOTTOM, 2 * PAIR_ROWS))],
        out_specs=pl.BlockSpec((SEQ_TILE, d_mix), lambda k: (jnp.maximum(k - 1, 0), 0)),
        out_shape=jax.ShapeDtypeStruct((n, d_mix), _BF16),
        scratch_shapes=[pltpu.VMEM((d_model, _OFF_LR), _BF16),
                        tile_bf16(d_model), tile_bf16(d_model),
                        tile_bf16(LANES),
                        tile_bf16(D_ATT), tile_bf16(D_ATT), tile_bf16(D_ATT),
                        tile_f32(D_ATT), tile_f32(2 * D_GLA_K), tile_bf16(D_GLA_V),
                        tile_f32(D_GLA_V), tile_f32(D_GLA_K)]
                       + projected
                       + [pltpu.VMEM((LEFT + SEQ_TILE, D_ATT), _BF16),
                          pltpu.VMEM(((LEFT + SEQ_TILE) // KEY_BLOCK, D_ATT, KEY_BLOCK), _BF16),
                          pltpu.VMEM((D_GLA_K, GLA_HEAD_V), _F32)],
        compiler_params=cparams(dimension_semantics=("arbitrary",)),
        name="proj_mixers",
    )(x2, x2, row(ln_in_g), row(ln_in_b), w_in, w_code, wg_b, row(b_gla_gate[0]), row(gla_norm_g[0]), bias)

    row_spec = lambda width: pl.BlockSpec((ROW_TILE, width), lambda i: (i, 0))
    out = pl.pallas_call(
        functools.partial(_epilogue_kernel, alpha=alpha),
        grid=(n // ROW_TILE,),
        in_specs=[row_spec(d_model), _const_spec((1, d_model)), _const_spec((1, d_model)),
                  row_spec(d_mix), row_spec(d_ple),
                  _const_spec((d_mix, d_model)), _const_spec((d_model, d_model)),
                  _const_spec((1, d_model)), _const_spec((d_ple, d_model)),
                  _const_spec((1, d_model)), _const_spec((1, d_model))],
        out_specs=row_spec(d_model),
        out_shape=jax.ShapeDtypeStruct((n, d_model), x.dtype),
        compiler_params=cparams(dimension_semantics=("parallel",)),
        name="out_proj_ple_norm",
    )(x2, row(ln_in_g), row(ln_in_b), mix, p2, w_out[0].astype(_BF16), w_ple_gate[0].astype(_BF16),
      row(b_ple_gate[0]), w_ple[0].astype(_BF16), row(ln_g[0]), row(ln_b[0]))
    return out.reshape(batch, seq, d_model)
```

```python
import functools
import math

import jax
import jax.numpy as jnp
from jax import lax
from jax.experimental import pallas as pl
from jax.experimental.pallas import tpu as pltpu

CHUNK = 64
N_ATT_HEADS = 8
ATT_HEAD_DIM = 64
D_ATT = N_ATT_HEADS * ATT_HEAD_DIM
LEFT_CHUNKS = 8
LEFT = LEFT_CHUNKS * CHUNK
BAND = LEFT + CHUNK
REL_CLIP = 128
N_GLA_HEADS = 4
GLA_HEAD_K = 64
GLA_HEAD_V = 128
D_GLA_K = N_GLA_HEADS * GLA_HEAD_K
D_GLA_V = N_GLA_HEADS * GLA_HEAD_V
GLA_LOW_RANK = 16
GLA_TAU = 16.0
LN_EPS = 1e-5
RMS_EPS = 1e-6
MASK_VALUE = -1e30
LOG2_E = math.log2(math.e)

LANES = 128
HEAD_PAIR = 2 * ATT_HEAD_DIM
assert HEAD_PAIR == LANES and 2 * GLA_HEAD_K == LANES
N_HEAD_PAIRS = N_ATT_HEADS // 2
PAIR_ROWS = 2 * CHUNK
KEY_SPAN = LEFT + PAIR_ROWS
KEY_BLOCK = LANES
N_SPAN_BLOCKS = KEY_SPAN // KEY_BLOCK
N_LEFT_BLOCKS = LEFT // KEY_BLOCK
assert PAIR_ROWS == KEY_BLOCK
BIAS_TOP = CHUNK
BIAS_BOTTOM = REL_CLIP + PAIR_ROWS
SUM_ROWS = 16
PV_KEY_TILE = 256
N_PV_TILES = -(-KEY_SPAN // PV_KEY_TILE)
assert N_PV_TILES == 3

_OFF_AG = 3 * D_ATT
_OFF_GQK = 4 * D_ATT
_OFF_GV = _OFF_GQK + 2 * D_GLA_K
_OFF_GG = _OFF_GV + D_GLA_V
_OFF_LR = _OFF_GG + D_GLA_V
D_IN_PROJ = _OFF_LR + GLA_LOW_RANK
PROJ_PIECE = 256
N_MAIN_PIECES = _OFF_LR // PROJ_PIECE
N_PROJ_PIECES = N_MAIN_PIECES + 2
PROJ_SLOTS_PER_PAIR = 4

ROW_TILE = 1024
EPILOGUE_SPLIT = 4
SEQ_TILE = 512
VMEM_LIMIT = 56 * 1024 * 1024

_F32 = jnp.float32
_BF16 = jnp.bfloat16
_NT = (((1,), (1,)), ((), ()))
_TN = (((0,), (0,)), ((), ()))


def _layer_norm(xf, g, b):
    mu = jnp.mean(xf, axis=-1, keepdims=True)
    xc = xf - mu
    var = jnp.mean(xc * xc, axis=-1, keepdims=True)
    return xc * lax.rsqrt(var + LN_EPS) * g + b


def _silu(x):
    return x * jax.nn.sigmoid(x)


def _dot(a, b):
    return jnp.dot(a, b, preferred_element_type=_F32)


def _proj_mix_kernel(x0_ref, x_ref, g_ref, b_ref, w32_ref, wg_ref, bg_ref, gn_ref, bias_ref,
                     mix_ref,
                     w_ref, wc_ref, hb_ref, nhb_ref, code_ref, nq_ref, nk_ref, nv_ref, nag_ref, ngqk_ref, ngv_ref, ngg_ref, nla_ref,
                     q_ref, ag_ref, gqk_ref, gv_ref, gg_ref, loga_ref,
                     k_hist, vt_hist, state_ref, *, seq_tile, tiles_per_seq):
    k = pl.program_id(0)
    s = lax.rem(k + tiles_per_seq - 1, tiles_per_seq)
    n_new_blocks = seq_tile // KEY_BLOCK
    next_refs = (nq_ref, nk_ref, nv_ref, nag_ref, ngqk_ref, ngv_ref, ngg_ref, nla_ref)

    @pl.when(k == 0)
    def _():
        for ref in next_refs + (k_hist, vt_hist, state_ref):
            ref[...] = jnp.zeros_like(ref)
        nhb_ref[...] = _layer_norm(x0_ref[...], g_ref[...], b_ref[...]).astype(_BF16)
        for c0 in range(0, _OFF_LR, PROJ_PIECE):
            w_ref[:, c0:c0 + PROJ_PIECE] = w32_ref[c0:c0 + PROJ_PIECE, :].T.astype(_BF16)
        code_t = jnp.concatenate([w32_ref[_OFF_LR:D_IN_PROJ, :],
                                  jnp.zeros((LANES - GLA_LOW_RANK, w32_ref.shape[1]), _F32)], axis=0)
        wc_ref[...] = code_t.T.astype(_BF16)

    @pl.when(s == 0)
    def _():
        k_hist[0:LEFT, :] = jnp.zeros((LEFT, D_ATT), _BF16)
        vt_hist[0:N_LEFT_BLOCKS] = jnp.zeros((N_LEFT_BLOCKS, D_ATT, KEY_BLOCK), _BF16)
        state_ref[...] = jnp.zeros_like(state_ref)

    @pl.when(s > 0)
    def _():
        k_hist[0:LEFT, :] = k_hist[seq_tile:seq_tile + LEFT, :]
        vt_hist[0:N_LEFT_BLOCKS] = vt_hist[n_new_blocks:n_new_blocks + N_LEFT_BLOCKS]

    k_hist[LEFT:LEFT + seq_tile, :] = nk_ref[...]
    for blk in range(n_new_blocks):
        vt_hist[N_LEFT_BLOCKS + blk] = nv_ref[blk * KEY_BLOCK:(blk + 1) * KEY_BLOCK, :].T
    for dst, src in ((q_ref, nq_ref), (ag_ref, nag_ref), (gqk_ref, ngqk_ref), (gv_ref, ngv_ref),
                     (gg_ref, ngg_ref), (loga_ref, nla_ref)):
        dst[...] = src[...]

    hb_ref[...] = nhb_ref[...]

    def norm_next(cp):
        slab = pl.ds(cp * PAIR_ROWS, PAIR_ROWS)
        nhb_ref[slab, :] = _layer_norm(x_ref[slab, :], g_ref[...], b_ref[...]).astype(_BF16)

    def proj_piece(i):
        if i == N_MAIN_PIECES:
            code_ref[...] = _dot(hb_ref[...], wc_ref[...]).astype(_BF16)
            return
        if i == N_MAIN_PIECES + 1:
            logit = _dot(code_ref[...], wg_ref[...]) + bg_ref[...]
            nla_ref[...] = jax.nn.log_sigmoid(logit) * (1.0 / GLA_TAU)
            return
        c0 = i * PROJ_PIECE
        acc = _dot(hb_ref[...], w_ref[:, c0:c0 + PROJ_PIECE])

        def put(ref, base, val):
            ref[:, c0 - base:c0 - base + PROJ_PIECE] = val.astype(ref.dtype)

        if c0 < D_ATT:
            put(nq_ref, 0, acc * (ATT_HEAD_DIM ** -0.5 * LOG2_E))
        elif c0 < 2 * D_ATT:
            put(nk_ref, D_ATT, acc)
        elif c0 < _OFF_AG:
            put(nv_ref, 2 * D_ATT, acc)
        elif c0 < _OFF_GQK:
            put(nag_ref, _OFF_AG, _silu(acc))
        elif c0 < _OFF_GQK + D_GLA_K:
            put(ngqk_ref, _OFF_GQK, acc * (GLA_HEAD_K ** -0.5))
        elif c0 < _OFF_GV:
            put(ngqk_ref, _OFF_GQK, acc)
        elif c0 < _OFF_GG:
            put(ngv_ref, _OFF_GV, acc)
        else:
            put(ngg_ref, _OFF_GG, _silu(acc) * gn_ref[:, c0 - _OFF_GG:c0 - _OFF_GG + PROJ_PIECE])

    lane = lax.broadcasted_iota(jnp.int32, (1, LANES), 1)
    low_half = lane < ATT_HEAD_DIM
    top_rows = lax.broadcasted_iota(jnp.int32, (LANES, 1), 0) < ATT_HEAD_DIM
    row_i = lax.broadcasted_iota(jnp.int32, (CHUNK, CHUNK), 0)
    col_i = lax.broadcasted_iota(jnp.int32, (CHUNK, CHUNK), 1)
    causal = row_i >= col_i
    chunk_row = lax.broadcasted_iota(jnp.int32, (CHUNK, 1), 0)
    ones_rows = jnp.ones((SUM_ROWS, KEY_SPAN), _BF16)

    def pair_body(cp, pieces):
        r0 = cp * PAIR_ROWS
        rows = pl.ds(r0, PAIR_ROWS)
        span = pl.ds(r0, KEY_SPAN)
        first_key = s * seq_tile + cp * PAIR_ROWS - LEFT
        pieces = list(pieces)

        def proj_next():
            piece = pieces.pop(0)
            if piece is not None:
                proj_piece(piece)

        def att_scores(j):
            cols = slice(j * LANES, (j + 1) * LANES)
            qp = q_ref[rows, cols]
            zero = jnp.zeros_like(qp)
            q_blk = jnp.concatenate([jnp.where(low_half, qp, zero), jnp.where(low_half, zero, qp)], axis=0)
            kp = k_hist[span, cols]
            sc = lax.dot_general(kp, q_blk, _NT, preferred_element_type=_F32)
            slabs = [sc[0:BIAS_TOP] + bias_ref[j, 0:BIAS_TOP, :],
                     sc[BIAS_TOP:KEY_SPAN - BIAS_BOTTOM],
                     sc[KEY_SPAN - BIAS_BOTTOM:] + bias_ref[j, BIAS_TOP:, :]]
            sc = jnp.concatenate(slabs, axis=0)
            blocks = [jnp.where(first_key + i * KEY_BLOCK >= 0, sc[i * KEY_BLOCK:(i + 1) * KEY_BLOCK, :], MASK_VALUE)
                      for i in range(N_SPAN_BLOCKS)]
            return jnp.concatenate(blocks, axis=0)

        def att_softmax(sc):
            m = jnp.max(sc, axis=0, keepdims=True)
            return jnp.exp2(sc - m).astype(_BF16)

        def att_out(j, pe):
            cols = slice(j * LANES, (j + 1) * LANES)
            partial = []

            def key_tile(t):
                k0, k1 = t * PV_KEY_TILE, min((t + 1) * PV_KEY_TILE, KEY_SPAN)
                vt = jnp.concatenate([vt_hist[cp + i, cols, :] for i in range(k0 // KEY_BLOCK, k1 // KEY_BLOCK)],
                                     axis=1)
                partial.append(_dot(jnp.concatenate([vt, ones_rows[:, k0:k1]], axis=0), pe[k0:k1, :]))

            def finish():
                ot = functools.reduce(jnp.add, partial)
                inv = 1.0 / ot[LANES:LANES + 1, :]
                num = jnp.where(top_rows, ot[0:LANES, 0:LANES], ot[0:LANES, LANES:2 * LANES])
                scale = jnp.where(top_rows, inv[:, 0:LANES], inv[:, LANES:2 * LANES])
                att = (num * scale).T
                mix_ref[rows, cols] = (att * ag_ref[rows, cols]).astype(_BF16)

            return key_tile, finish

        def gla_cumsum(c):
            crow = pl.ds(cp * PAIR_ROWS + c * CHUNK, CHUNK)
            cum = loga_ref[crow, :]
            shift = 1
            while shift < CHUNK:
                cum = cum + jnp.where(chunk_row >= shift, pltpu.roll(cum, shift, axis=0), 0.0)
                shift *= 2
            return cum

        def gla_scores(c, cum):
            crow = pl.ds(cp * PAIR_ROWS + c * CHUNK, CHUNK)
            cum_end = cum[CHUNK - 1:CHUNK, :]
            e_pos = jnp.exp(cum)
            e_neg = jnp.exp(-cum)
            gq = gqk_ref[crow, 0:D_GLA_K]
            gk = gqk_ref[crow, D_GLA_K:2 * D_GLA_K]
            q_fwd = gq * e_pos
            q_bwd = gq * e_neg
            k_fwd = (gk * e_pos).astype(_BF16)
            k_bwd = (gk * e_neg).astype(_BF16)
            k_end = (gk * jnp.exp(cum_end - cum)).astype(_BF16)
            decay = jnp.exp(cum_end)
            parts = []
            for g in range(N_GLA_HEADS // 2):
                pcols = slice(g * LANES, (g + 1) * LANES)
                prows = slice(g * LANES, (g + 1) * LANES)
                state = state_ref[prows, :]
                state_b = state.astype(_BF16)
                for e in range(2):
                    sel = low_half if e == 0 else jnp.logical_not(low_half)
                    qf = jnp.where(sel, q_fwd[:, pcols], 0.0).astype(_BF16)
                    qb = jnp.where(sel, q_bwd[:, pcols], 0.0).astype(_BF16)
                    a_causal = lax.dot_general(qf, k_bwd[:, pcols], _NT, preferred_element_type=_F32)
                    a_anti = lax.dot_general(qb, k_fwd[:, pcols], _NT, preferred_element_type=_F32)
                    parts.append((a_causal, a_anti, _dot(qf, state_b)))
                v_pair = gv_ref[crow, 2 * g * GLA_HEAD_V:(2 * g + 2) * GLA_HEAD_V]
                kv = lax.dot_general(k_end[:, pcols], v_pair, _TN, preferred_element_type=_F32)
                new_rows = jnp.where(top_rows, kv[:, 0:GLA_HEAD_V], kv[:, GLA_HEAD_V:])
                decay_rows = jnp.broadcast_to(decay[:, pcols], (LANES, LANES)).T
                state_ref[prows, :] = decay_rows * state + new_rows
            return parts

        def gla_out(c, parts):
            crow = pl.ds(cp * PAIR_ROWS + c * CHUNK, CHUNK)
            for h in range(N_GLA_HEADS):
                a_causal, a_anti, o_inter = parts[h]
                hv = slice(h * GLA_HEAD_V, (h + 1) * GLA_HEAD_V)
                att = jnp.where(causal, a_causal, a_anti).astype(_BF16)
                o = _dot(att, gv_ref[crow, hv]) + o_inter
                o = o * lax.rsqrt(jnp.mean(o * o, axis=-1, keepdims=True) + RMS_EPS)
                mix_ref[crow, D_ATT + h * GLA_HEAD_V:D_ATT + (h + 1) * GLA_HEAD_V] = (
                    o * gg_ref[crow, hv]).astype(_BF16)

        cum0 = gla_cumsum(0)
        cum1 = gla_cumsum(1)
        sc0 = att_scores(0)
        sc1 = att_scores(1)
        g0 = gla_scores(0, cum0)
        pv, finish = att_out(0, att_softmax(sc0))
        pv(0)
        proj_next()
        pv(1)
        sc2 = att_scores(2)
        pv(2)
        finish()
        gla_out(0, g0)
        g1 = gla_scores(1, cum1)
        pv, finish = att_out(1, att_softmax(sc1))
        pv(0)
        proj_next()
        pv(1)
        sc3 = att_scores(3)
        pv(2)
        finish()
        gla_out(1, g1)
        norm_next(cp)
        for j in (2, 3):
            pv, finish = att_out(j, att_softmax(sc2 if j == 2 else sc3))
            pv(0)
            proj_next()
            pv(1)
            pv(2)
            finish()
        assert not pieces

    n_pairs = seq_tile // PAIR_ROWS
    slots = PROJ_SLOTS_PER_PAIR * n_pairs
    heavy = [c // PROJ_PIECE for c in range(_OFF_AG, _OFF_GQK, PROJ_PIECE)] \
        + [c // PROJ_PIECE for c in range(_OFF_GG, _OFF_LR, PROJ_PIECE)] + [N_MAIN_PIECES + 1]
    light = [N_MAIN_PIECES] + [i for i in range(N_MAIN_PIECES) if i not in heavy]
    schedule = []
    for slot in range(slots):
        first = heavy if slot % PROJ_SLOTS_PER_PAIR == 0 else light
        source = first or heavy or light
        schedule.append(source.pop(0) if source else None)
    assert not heavy and not light
    for cp in range(n_pairs):
        pair_body(cp, schedule[cp * PROJ_SLOTS_PER_PAIR:(cp + 1) * PROJ_SLOTS_PER_PAIR])


def _epilogue_kernel(x_ref, g_ref, b_ref, mix_ref, p_ref, wo_ref, wpg_ref, bpg_ref, wp_ref,
                     og_ref, ob_ref, out_ref, *, alpha):
    half = x_ref.shape[0] // EPILOGUE_SPLIT
    halves = [pl.ds(i * half, half) for i in range(EPILOGUE_SPLIT)]

    def residual(rows):
        h = _layer_norm(x_ref[rows, :], g_ref[...], b_ref[...])
        return alpha * h + _dot(mix_ref[rows, :], wo_ref[...])

    def gate_and_embed(rows, r):
        logit = _dot(r.astype(_BF16), wpg_ref[...])
        return logit, _dot(p_ref[rows, :].astype(_BF16), wp_ref[...])

    def finish(rows, r, logit, ple):
        r = r + jax.nn.sigmoid(logit + bpg_ref[...]) * ple
        out_ref[rows, :] = _layer_norm(r, og_ref[...], ob_ref[...])

    rs = [residual(rows) for rows in halves]
    ts = [gate_and_embed(rows, r) for rows, r in zip(halves, rs)]
    for rows, r, t in zip(halves, rs, ts):
        finish(rows, r, *t)


def _const_spec(shape):
    zeros = (0,) * len(shape)
    return pl.BlockSpec(shape, lambda *_: zeros, pipeline_mode=pl.Buffered(1))


def _rel_bias_span(rel_table):
    table = rel_table.astype(_F32)
    heads = table.shape[0]
    near = jnp.flip(table[:, REL_CLIP - (CHUNK - 1):2 * REL_CLIP], axis=1) - table[:, 2 * REL_CLIP:]
    ext = jnp.concatenate([jnp.zeros((heads, BAND - REL_CLIP), _F32), near], axis=1) * LOG2_E
    width = ext.shape[1] + 1
    tiled = jnp.tile(jnp.pad(ext, ((0, 0), (0, 1))), (1, CHUNK))[:, :CHUNK * (width - 1)]
    band = tiled.reshape(heads, CHUNK, width - 1)[:, :, CHUNK - 1:CHUNK - 1 + BAND]
    pad = lambda lo, hi: jnp.pad(band, ((0, 0), (0, 0), (lo, hi)), constant_values=MASK_VALUE)
    both = jnp.stack([pad(0, CHUNK), pad(CHUNK, 0)], axis=1)
    both = both.reshape(N_HEAD_PAIRS, 2, 2, CHUNK, KEY_SPAN)
    full = both.transpose(0, 4, 1, 2, 3).reshape(N_HEAD_PAIRS, KEY_SPAN, 2 * PAIR_ROWS)
    return jnp.concatenate([full[:, :BIAS_TOP], full[:, KEY_SPAN - BIAS_BOTTOM:]], axis=1)


def kernel(x, p, ln_in_g, ln_in_b, w_in, w_gla_gate, b_gla_gate, rel_bias,
           gla_norm_g, w_out, w_ple, w_ple_gate, b_ple_gate, ln_g, ln_b):
    batch, seq, d_model = x.shape
    depth = w_in.shape[0]
    assert depth == 1, "single-layer stack only"
    assert w_in.shape[2] == D_IN_PROJ
    assert seq % SEQ_TILE == 0 and SEQ_TILE % PAIR_ROWS == 0 and SEQ_TILE >= LEFT
    n = batch * seq
    assert n % ROW_TILE == 0
    d_ple = p.shape[-1]
    d_mix = D_ATT + D_GLA_V
    alpha = (2.0 * depth) ** 0.25

    x2 = x.reshape(n, d_model)
    p2 = p.reshape(n, d_ple)
    row = lambda v: v.reshape(1, -1).astype(_F32)
    wg_b = jnp.pad(w_gla_gate[0], ((0, LANES - GLA_LOW_RANK), (0, 0))).astype(_BF16)
    bias = _rel_bias_span(rel_bias[0])

    cparams = functools.partial(pltpu.CompilerParams, vmem_limit_bytes=VMEM_LIMIT)

    tiles_per_seq = seq // SEQ_TILE
    n_tiles = batch * tiles_per_seq
    tile_f32 = lambda width: pltpu.VMEM((SEQ_TILE, width), _F32)
    tile_bf16 = lambda width: pltpu.VMEM((SEQ_TILE, width), _BF16)
    projected = [tile_bf16(D_ATT), tile_f32(D_ATT), tile_f32(2 * D_GLA_K), tile_bf16(D_GLA_V),
                 tile_f32(D_GLA_V), tile_f32(D_GLA_K)]
    mix = pl.pallas_call(
        functools.partial(_proj_mix_kernel, seq_tile=SEQ_TILE, tiles_per_seq=tiles_per_seq),
        grid=(n_tiles + 1,),
        in_specs=[_const_spec((SEQ_TILE, d_model)),
                  pl.BlockSpec((SEQ_TILE, d_model), lambda k: (jnp.minimum(k + 1, n_tiles - 1), 0)),
                  _const_spec((1, d_model)), _const_spec((1, d_model)),
                  pl.BlockSpec((None, D_IN_PROJ, d_model), lambda k: (0, 0, 0), pipeline_mode=pl.Buffered(1)),
                  _const_spec((LANES, D_GLA_K)), _const_spec((1, D_GLA_K)), _const_spec((1, D_GLA_V)),
                  _const_spec((N_HEAD_PAIRS, BIAS_TOP + BIAS_BOTTOM, 2 * PAIR_ROWS))],
        out_specs=pl.BlockSpec((SEQ_TILE, d_mix), lambda k: (jnp.maximum(k - 1, 0), 0)),
        out_shape=jax.ShapeDtypeStruct((n, d_mix), _BF16),
        scratch_shapes=[pltpu.VMEM((d_model, _OFF_LR), _BF16),
                        pltpu.VMEM((d_model, LANES), _BF16),
                        tile_bf16(d_model), tile_bf16(d_model),
                        tile_bf16(LANES),
                        tile_bf16(D_ATT), tile_bf16(D_ATT), tile_bf16(D_ATT),
                        tile_f32(D_ATT), tile_f32(2 * D_GLA_K), tile_bf16(D_GLA_V),
                        tile_f32(D_GLA_V), tile_f32(D_GLA_K)]
                       + projected
                       + [pltpu.VMEM((LEFT + SEQ_TILE, D_ATT), _BF16),
                          pltpu.VMEM(((LEFT + SEQ_TILE) // KEY_BLOCK, D_ATT, KEY_BLOCK), _BF16),
                          pltpu.VMEM((D_GLA_K, GLA_HEAD_V), _F32)],
        compiler_params=cparams(dimension_semantics=("arbitrary",)),
        name="proj_mixers",
    )(x2, x2, row(ln_in_g), row(ln_in_b), jnp.swapaxes(w_in, 1, 2), wg_b, row(b_gla_gate[0]), row(gla_norm_g[0]), bias)

    row_spec = lambda width: pl.BlockSpec((ROW_TILE, width), lambda i: (i, 0))
    out = pl.pallas_call(
        functools.partial(_epilogue_kernel, alpha=alpha),
        grid=(n // ROW_TILE,),
        in_specs=[row_spec(d_model), _const_spec((1, d_model)), _const_spec((1, d_model)),
                  row_spec(d_mix), row_spec(d_ple),
                  _const_spec((d_mix, d_model)), _const_spec((d_model, d_model)),
                  _const_spec((1, d_model)), _const_spec((d_ple, d_model)),
                  _const_spec((1, d_model)), _const_spec((1, d_model))],
        out_specs=row_spec(d_model),
        out_shape=jax.ShapeDtypeStruct((n, d_model), x.dtype),
        compiler_params=cparams(dimension_semantics=("parallel",)),
        name="out_proj_ple_norm",
    )(x2, row(ln_in_g), row(ln_in_b), mix, p2, w_out[0].astype(_BF16), w_ple_gate[0].astype(_BF16),
      row(b_ple_gate[0]), w_ple[0].astype(_BF16), row(ln_g[0]), row(ln_b[0]))
    return out.reshape(batch, seq, d_model)
```

```python
import functools
import math

import jax
import jax.numpy as jnp
from jax import lax
from jax.experimental import pallas as pl
from jax.experimental.pallas import tpu as pltpu

CHUNK = 64
N_ATT_HEADS = 8
ATT_HEAD_DIM = 64
D_ATT = N_ATT_HEADS * ATT_HEAD_DIM
LEFT_CHUNKS = 8
LEFT = LEFT_CHUNKS * CHUNK
BAND = LEFT + CHUNK
REL_CLIP = 128
N_GLA_HEADS = 4
GLA_HEAD_K = 64
GLA_HEAD_V = 128
D_GLA_K = N_GLA_HEADS * GLA_HEAD_K
D_GLA_V = N_GLA_HEADS * GLA_HEAD_V
GLA_LOW_RANK = 16
GLA_TAU = 16.0
LN_EPS = 1e-5
RMS_EPS = 1e-6
MASK_VALUE = -1e30
LOG2_E = math.log2(math.e)

LANES = 128
HEAD_PAIR = 2 * ATT_HEAD_DIM
assert HEAD_PAIR == LANES and 2 * GLA_HEAD_K == LANES
N_HEAD_PAIRS = N_ATT_HEADS // 2
PAIR_ROWS = 2 * CHUNK
KEY_SPAN = LEFT + PAIR_ROWS
KEY_BLOCK = LANES
N_SPAN_BLOCKS = KEY_SPAN // KEY_BLOCK
N_LEFT_BLOCKS = LEFT // KEY_BLOCK
assert PAIR_ROWS == KEY_BLOCK
BIAS_TOP = CHUNK
BIAS_BOTTOM = REL_CLIP + PAIR_ROWS
SUM_ROWS = 16
PV_KEY_TILE = 256
N_PV_TILES = -(-KEY_SPAN // PV_KEY_TILE)
assert N_PV_TILES == 3

_OFF_AG = 3 * D_ATT
_OFF_GQK = 4 * D_ATT
_OFF_GV = _OFF_GQK + 2 * D_GLA_K
_OFF_GG = _OFF_GV + D_GLA_V
_OFF_LR = _OFF_GG + D_GLA_V
D_IN_PROJ = _OFF_LR + GLA_LOW_RANK
PROJ_PIECE = 256
N_MAIN_PIECES = _OFF_LR // PROJ_PIECE
N_PROJ_PIECES = N_MAIN_PIECES + 2
PROJ_SLOTS_PER_PAIR = 4

ROW_TILE = 1024
EPILOGUE_SPLIT = 4
SEQ_TILE = 512
VMEM_LIMIT = 56 * 1024 * 1024

_F32 = jnp.float32
_BF16 = jnp.bfloat16
_NT = (((1,), (1,)), ((), ()))
_TN = (((0,), (0,)), ((), ()))


def _layer_norm(xf, g, b):
    mu = jnp.mean(xf, axis=-1, keepdims=True)
    xc = xf - mu
    var = jnp.mean(xc * xc, axis=-1, keepdims=True)
    return xc * lax.rsqrt(var + LN_EPS) * g + b


def _silu(x):
    return x * jax.nn.sigmoid(x)


def _dot(a, b):
    return jnp.dot(a, b, preferred_element_type=_F32)


def _proj_mix_kernel(x0_ref, x_ref, g_ref, b_ref, w32_ref, wg_ref, bg_ref, gn_ref, bias_ref,
                     mix_ref,
                     w_ref, wc_ref, hb_ref, nhb_ref, code_ref, nq_ref, nk_ref, nv_ref, nag_ref, ngqk_ref, ngv_ref, ngg_ref, nla_ref,
                     q_ref, ag_ref, gqk_ref, gv_ref, gg_ref, loga_ref,
                     k_hist, vt_hist, state_ref, *, seq_tile, tiles_per_seq):
    k = pl.program_id(0)
    s = lax.rem(k + tiles_per_seq - 1, tiles_per_seq)
    n_new_blocks = seq_tile // KEY_BLOCK
    next_refs = (nq_ref, nk_ref, nv_ref, nag_ref, ngqk_ref, ngv_ref, ngg_ref, nla_ref)

    @pl.when(k == 0)
    def _():
        for ref in next_refs + (k_hist, vt_hist, state_ref):
            ref[...] = jnp.zeros_like(ref)
        nhb_ref[...] = _layer_norm(x0_ref[...], g_ref[...], b_ref[...]).astype(_BF16)
        for c0 in range(0, _OFF_LR, PROJ_PIECE):
            w_ref[:, c0:c0 + PROJ_PIECE] = w32_ref[c0:c0 + PROJ_PIECE, :].T.astype(_BF16)
        code_t = jnp.concatenate([w32_ref[_OFF_LR:D_IN_PROJ, :],
                                  jnp.zeros((LANES - GLA_LOW_RANK, w32_ref.shape[1]), _F32)], axis=0)
        wc_ref[...] = code_t.T.astype(_BF16)

    @pl.when(s == 0)
    def _():
        k_hist[0:LEFT, :] = jnp.zeros((LEFT, D_ATT), _BF16)
        vt_hist[0:N_LEFT_BLOCKS] = jnp.zeros((N_LEFT_BLOCKS, D_ATT, KEY_BLOCK), _BF16)
        state_ref[...] = jnp.zeros_like(state_ref)

    @pl.when(s > 0)
    def _():
        k_hist[0:LEFT, :] = k_hist[seq_tile:seq_tile + LEFT, :]
        vt_hist[0:N_LEFT_BLOCKS] = vt_hist[n_new_blocks:n_new_blocks + N_LEFT_BLOCKS]

    k_hist[LEFT:LEFT + seq_tile, :] = nk_ref[...]
    for blk in range(n_new_blocks):
        vt_hist[N_LEFT_BLOCKS + blk] = nv_ref[blk * KEY_BLOCK:(blk + 1) * KEY_BLOCK, :].T
    for dst, src in ((q_ref, nq_ref), (ag_ref, nag_ref), (gqk_ref, ngqk_ref), (gv_ref, ngv_ref),
                     (gg_ref, ngg_ref), (loga_ref, nla_ref)):
        dst[...] = src[...]

    hb_ref[...] = nhb_ref[...]

    def norm_next(cp):
        slab = pl.ds(cp * PAIR_ROWS, PAIR_ROWS)
        nhb_ref[slab, :] = _layer_norm(x_ref[slab, :], g_ref[...], b_ref[...]).astype(_BF16)

    def proj_piece(i):
        if i == N_MAIN_PIECES:
            code_ref[...] = _dot(hb_ref[...], wc_ref[...]).astype(_BF16)
            return
        if i == N_MAIN_PIECES + 1:
            logit = _dot(code_ref[...], wg_ref[...]) + bg_ref[...]
            nla_ref[...] = jax.nn.log_sigmoid(logit) * (1.0 / GLA_TAU)
            return
        c0 = i * PROJ_PIECE
        acc = _dot(hb_ref[...], w_ref[:, c0:c0 + PROJ_PIECE])

        def put(ref, base, val):
            ref[:, c0 - base:c0 - base + PROJ_PIECE] = val.astype(ref.dtype)

        if c0 < D_ATT:
            put(nq_ref, 0, acc * (ATT_HEAD_DIM ** -0.5 * LOG2_E))
        elif c0 < 2 * D_ATT:
            put(nk_ref, D_ATT, acc)
        elif c0 < _OFF_AG:
            put(nv_ref, 2 * D_ATT, acc)
        elif c0 < _OFF_GQK:
            put(nag_ref, _OFF_AG, _silu(acc))
        elif c0 < _OFF_GQK + D_GLA_K:
            put(ngqk_ref, _OFF_GQK, acc * (GLA_HEAD_K ** -0.5))
        elif c0 < _OFF_GV:
            put(ngqk_ref, _OFF_GQK, acc)
        elif c0 < _OFF_GG:
            put(ngv_ref, _OFF_GV, acc)
        else:
            put(ngg_ref, _OFF_GG, _silu(acc) * gn_ref[:, c0 - _OFF_GG:c0 - _OFF_GG + PROJ_PIECE])

    lane = lax.broadcasted_iota(jnp.int32, (1, LANES), 1)
    low_half = lane < ATT_HEAD_DIM
    top_rows = lax.broadcasted_iota(jnp.int32, (LANES, 1), 0) < ATT_HEAD_DIM
    row_i = lax.broadcasted_iota(jnp.int32, (CHUNK, CHUNK), 0)
    col_i = lax.broadcasted_iota(jnp.int32, (CHUNK, CHUNK), 1)
    causal = row_i >= col_i
    chunk_row = lax.broadcasted_iota(jnp.int32, (CHUNK, 1), 0)
    ones_rows = jnp.ones((SUM_ROWS, KEY_SPAN), _BF16)

    def pair_body(cp, pieces):
        r0 = cp * PAIR_ROWS
        rows = pl.ds(r0, PAIR_ROWS)
        span = pl.ds(r0, KEY_SPAN)
        first_key = s * seq_tile + cp * PAIR_ROWS - LEFT
        pieces = list(pieces)

        def proj_next():
            piece = pieces.pop(0)
            if piece is not None:
                proj_piece(piece)

        def att_scores(j):
            cols = slice(j * LANES, (j + 1) * LANES)
            qp = q_ref[rows, cols]
            zero = jnp.zeros_like(qp)
            q_blk = jnp.concatenate([jnp.where(low_half, qp, zero), jnp.where(low_half, zero, qp)], axis=0)
            kp = k_hist[span, cols]
            sc = lax.dot_general(kp, q_blk, _NT, preferred_element_type=_F32)
            slabs = [sc[0:BIAS_TOP] + bias_ref[j, 0:BIAS_TOP, :],
                     sc[BIAS_TOP:KEY_SPAN - BIAS_BOTTOM],
                     sc[KEY_SPAN - BIAS_BOTTOM:] + bias_ref[j, BIAS_TOP:, :]]
            sc = jnp.concatenate(slabs, axis=0)
            blocks = [sc[i * KEY_BLOCK:(i + 1) * KEY_BLOCK, :] for i in range(N_SPAN_BLOCKS)]
            for i in range(N_LEFT_BLOCKS - cp):
                blocks[i] = jnp.where(first_key + i * KEY_BLOCK >= 0, blocks[i], MASK_VALUE)
            return jnp.concatenate(blocks, axis=0)

        def att_softmax(sc):
            m = jnp.max(sc, axis=0, keepdims=True)
            return jnp.exp2(sc - m).astype(_BF16)

        def att_out(j, pe):
            cols = slice(j * LANES, (j + 1) * LANES)
            partial = []

            def key_tile(t):
                k0, k1 = t * PV_KEY_TILE, min((t + 1) * PV_KEY_TILE, KEY_SPAN)
                vt = jnp.concatenate([vt_hist[cp + i, cols, :] for i in range(k0 // KEY_BLOCK, k1 // KEY_BLOCK)],
                                     axis=1)
                partial.append(_dot(jnp.concatenate([vt, ones_rows[:, k0:k1]], axis=0), pe[k0:k1, :]))

            def finish():
                ot = functools.reduce(jnp.add, partial)
                inv = 1.0 / ot[LANES:LANES + 1, :]
                num = jnp.where(top_rows, ot[0:LANES, 0:LANES], ot[0:LANES, LANES:2 * LANES])
                scale = jnp.where(top_rows, inv[:, 0:LANES], inv[:, LANES:2 * LANES])
                att = (num * scale).T
                mix_ref[rows, cols] = (att * ag_ref[rows, cols]).astype(_BF16)

            return key_tile, finish

        def gla_cumsum(c):
            crow = pl.ds(cp * PAIR_ROWS + c * CHUNK, CHUNK)
            cum = loga_ref[crow, :]
            shift = 1
            while shift < CHUNK:
                cum = cum + jnp.where(chunk_row >= shift, pltpu.roll(cum, shift, axis=0), 0.0)
                shift *= 2
            return cum

        def gla_scores(c, cum):
            crow = pl.ds(cp * PAIR_ROWS + c * CHUNK, CHUNK)
            cum_end = cum[CHUNK - 1:CHUNK, :]
            e_pos = jnp.exp(cum)
            e_neg = jnp.exp(-cum)
            gq = gqk_ref[crow, 0:D_GLA_K]
            gk = gqk_ref[crow, D_GLA_K:2 * D_GLA_K]
            q_fwd = gq * e_pos
            q_bwd = gq * e_neg
            k_fwd = (gk * e_pos).astype(_BF16)
            k_bwd = (gk * e_neg).astype(_BF16)
            k_end = (gk * jnp.exp(cum_end - cum)).astype(_BF16)
            decay = jnp.exp(cum_end)
            parts = []
            for g in range(N_GLA_HEADS // 2):
                pcols = slice(g * LANES, (g + 1) * LANES)
                prows = slice(g * LANES, (g + 1) * LANES)
                state = state_ref[prows, :]
                k_bwd_t = jnp.concatenate([k_bwd[:, pcols], jnp.zeros((LANES - CHUNK, LANES), _BF16)], axis=0).T
                state_and_keys = jnp.concatenate([state.astype(_BF16), k_bwd_t], axis=1)
                for e in range(2):
                    sel = low_half if e == 0 else jnp.logical_not(low_half)
                    qf = jnp.where(sel, q_fwd[:, pcols], 0.0).astype(_BF16)
                    qb = jnp.where(sel, q_bwd[:, pcols], 0.0).astype(_BF16)
                    both = _dot(qf, state_and_keys)
                    a_causal = both[:, GLA_HEAD_V:GLA_HEAD_V + CHUNK]
                    a_anti = lax.dot_general(qb, k_fwd[:, pcols], _NT, preferred_element_type=_F32)
                    parts.append((a_causal, a_anti, both[:, 0:GLA_HEAD_V]))
                v_pair = gv_ref[crow, 2 * g * GLA_HEAD_V:(2 * g + 2) * GLA_HEAD_V]
                kv = lax.dot_general(k_end[:, pcols], v_pair, _TN, preferred_element_type=_F32)
                new_rows = jnp.where(top_rows, kv[:, 0:GLA_HEAD_V], kv[:, GLA_HEAD_V:])
                decay_rows = jnp.broadcast_to(decay[:, pcols], (LANES, LANES)).T
                state_ref[prows, :] = decay_rows * state + new_rows
            return parts

        def gla_out(c, parts):
            crow = pl.ds(cp * PAIR_ROWS + c * CHUNK, CHUNK)
            for h in range(N_GLA_HEADS):
                a_causal, a_anti, o_inter = parts[h]
                hv = slice(h * GLA_HEAD_V, (h + 1) * GLA_HEAD_V)
                att = jnp.where(causal, a_causal, a_anti).astype(_BF16)
                o = _dot(att, gv_ref[crow, hv]) + o_inter
                o = o * lax.rsqrt(jnp.mean(o * o, axis=-1, keepdims=True) + RMS_EPS)
                mix_ref[crow, D_ATT + h * GLA_HEAD_V:D_ATT + (h + 1) * GLA_HEAD_V] = (
                    o * gg_ref[crow, hv]).astype(_BF16)

        cum0 = gla_cumsum(0)
        cum1 = gla_cumsum(1)
        sc0 = att_scores(0)
        sc1 = att_scores(1)
        g0 = gla_scores(0, cum0)
        pv, finish = att_out(0, att_softmax(sc0))
        pv(0)
        proj_next()
        pv(1)
        sc2 = att_scores(2)
        pv(2)
        finish()
        gla_out(0, g0)
        g1 = gla_scores(1, cum1)
        pv, finish = att_out(1, att_softmax(sc1))
        pv(0)
        proj_next()
        pv(1)
        sc3 = att_scores(3)
        pv(2)
        finish()
        gla_out(1, g1)
        norm_next(cp)
        for j in (2, 3):
            pv, finish = att_out(j, att_softmax(sc2 if j == 2 else sc3))
            pv(0)
            proj_next()
            pv(1)
            pv(2)
            finish()
        assert not pieces

    n_pairs = seq_tile // PAIR_ROWS
    slots = PROJ_SLOTS_PER_PAIR * n_pairs
    heavy = [c // PROJ_PIECE for c in range(_OFF_AG, _OFF_GQK, PROJ_PIECE)] \
        + [c // PROJ_PIECE for c in range(_OFF_GG, _OFF_LR, PROJ_PIECE)] + [N_MAIN_PIECES + 1]
    light = [N_MAIN_PIECES] + [i for i in range(N_MAIN_PIECES) if i not in heavy]
    schedule = []
    for slot in range(slots):
        first = heavy if slot % PROJ_SLOTS_PER_PAIR == 0 else light
        source = first or heavy or light
        schedule.append(source.pop(0) if source else None)
    assert not heavy and not light
    for cp in range(n_pairs):
        pair_body(cp, schedule[cp * PROJ_SLOTS_PER_PAIR:(cp + 1) * PROJ_SLOTS_PER_PAIR])


def _epilogue_kernel(x_ref, g_ref, b_ref, mix_ref, p_ref, wo_ref, wpg_ref, bpg_ref, wp_ref,
                     og_ref, ob_ref, out_ref, *, alpha):
    half = x_ref.shape[0] // EPILOGUE_SPLIT
    halves = [pl.ds(i * half, half) for i in range(EPILOGUE_SPLIT)]

    def residual(rows):
        h = _layer_norm(x_ref[rows, :], g_ref[...], b_ref[...])
        return alpha * h + _dot(mix_ref[rows, :], wo_ref[...])

    def gate_and_embed(rows, r):
        logit = _dot(r.astype(_BF16), wpg_ref[...])
        return logit, _dot(p_ref[rows, :].astype(_BF16), wp_ref[...])

    def finish(rows, r, logit, ple):
        r = r + jax.nn.sigmoid(logit + bpg_ref[...]) * ple
        out_ref[rows, :] = _layer_norm(r, og_ref[...], ob_ref[...])

    rs = [residual(rows) for rows in halves]
    ts = [gate_and_embed(rows, r) for rows, r in zip(halves, rs)]
    for rows, r, t in zip(halves, rs, ts):
        finish(rows, r, *t)


def _const_spec(shape):
    zeros = (0,) * len(shape)
    return pl.BlockSpec(shape, lambda *_: zeros, pipeline_mode=pl.Buffered(1))


def _rel_bias_span(rel_table):
    table = rel_table.astype(_F32)
    heads = table.shape[0]
    near = jnp.flip(table[:, REL_CLIP - (CHUNK - 1):2 * REL_CLIP], axis=1) - table[:, 2 * REL_CLIP:]
    ext = jnp.concatenate([jnp.zeros((heads, BAND - REL_CLIP), _F32), near], axis=1) * LOG2_E
    width = ext.shape[1] + 1
    tiled = jnp.tile(jnp.pad(ext, ((0, 0), (0, 1))), (1, CHUNK))[:, :CHUNK * (width - 1)]
    band = tiled.reshape(heads, CHUNK, width - 1)[:, :, CHUNK - 1:CHUNK - 1 + BAND]
    pad = lambda lo, hi: jnp.pad(band, ((0, 0), (0, 0), (lo, hi)), constant_values=MASK_VALUE)
    both = jnp.stack([pad(0, CHUNK), pad(CHUNK, 0)], axis=1)
    both = both.reshape(N_HEAD_PAIRS, 2, 2, CHUNK, KEY_SPAN)
    full = both.transpose(0, 4, 1, 2, 3).reshape(N_HEAD_PAIRS, KEY_SPAN, 2 * PAIR_ROWS)
    return jnp.concatenate([full[:, :BIAS_TOP], full[:, KEY_SPAN - BIAS_BOTTOM:]], axis=1)


def kernel(x, p, ln_in_g, ln_in_b, w_in, w_gla_gate, b_gla_gate, rel_bias,
           gla_norm_g, w_out, w_ple, w_ple_gate, b_ple_gate, ln_g, ln_b):
    batch, seq, d_model = x.shape
    depth = w_in.shape[0]
    assert depth == 1, "single-layer stack only"
    assert w_in.shape[2] == D_IN_PROJ
    assert seq % SEQ_TILE == 0 and SEQ_TILE % PAIR_ROWS == 0 and SEQ_TILE >= LEFT
    n = batch * seq
    assert n % ROW_TILE == 0
    d_ple = p.shape[-1]
    d_mix = D_ATT + D_GLA_V
    alpha = (2.0 * depth) ** 0.25

    x2 = x.reshape(n, d_model)
    p2 = p.reshape(n, d_ple)
    row = lambda v: v.reshape(1, -1).astype(_F32)
    wg_b = jnp.pad(w_gla_gate[0], ((0, LANES - GLA_LOW_RANK), (0, 0))).astype(_BF16)
    bias = _rel_bias_span(rel_bias[0])

    cparams = functools.partial(pltpu.CompilerParams, vmem_limit_bytes=VMEM_LIMIT)

    tiles_per_seq = seq // SEQ_TILE
    n_tiles = batch * tiles_per_seq
    tile_f32 = lambda width: pltpu.VMEM((SEQ_TILE, width), _F32)
    tile_bf16 = lambda width: pltpu.VMEM((SEQ_TILE, width), _BF16)
    projected = [tile_bf16(D_ATT), tile_f32(D_ATT), tile_f32(2 * D_GLA_K), tile_bf16(D_GLA_V),
                 tile_f32(D_GLA_V), tile_f32(D_GLA_K)]
    mix = pl.pallas_call(
        functools.partial(_proj_mix_kernel, seq_tile=SEQ_TILE, tiles_per_seq=tiles_per_seq),
        grid=(n_tiles + 1,),
        in_specs=[_const_spec((SEQ_TILE, d_model)),
                  pl.BlockSpec((SEQ_TILE, d_model), lambda k: (jnp.minimum(k + 1, n_tiles - 1), 0)),
                  _const_spec((1, d_model)), _const_spec((1, d_model)),
                  pl.BlockSpec((None, D_IN_PROJ, d_model), lambda k: (0, 0, 0), pipeline_mode=pl.Buffered(1)),
                  _const_spec((LANES, D_GLA_K)), _const_spec((1, D_GLA_K)), _const_spec((1, D_GLA_V)),
                  _const_spec((N_HEAD_PAIRS, BIAS_TOP + BIAS_BOTTOM, 2 * PAIR_ROWS))],
        out_specs=pl.BlockSpec((SEQ_TILE, d_mix), lambda k: (jnp.maximum(k - 1, 0), 0)),
        out_shape=jax.ShapeDtypeStruct((n, d_mix), _BF16),
        scratch_shapes=[pltpu.VMEM((d_model, _OFF_LR), _BF16),
                        pltpu.VMEM((d_model, LANES), _BF16),
                        tile_bf16(d_model), tile_bf16(d_model),
                        tile_bf16(LANES),
                        tile_bf16(D_ATT), tile_bf16(D_ATT), tile_bf16(D_ATT),
                        tile_f32(D_ATT), tile_f32(2 * D_GLA_K), tile_bf16(D_GLA_V),
                        tile_f32(D_GLA_V), tile_f32(D_GLA_K)]
                       + projected
                       + [pltpu.VMEM((LEFT + SEQ_TILE, D_ATT), _BF16),
                          pltpu.VMEM(((LEFT + SEQ_TILE) // KEY_BLOCK, D_ATT, KEY_BLOCK), _BF16),
                          pltpu.VMEM((D_GLA_K, GLA_HEAD_V), _F32)],
        compiler_params=cparams(dimension_semantics=("arbitrary",)),
        name="proj_mixers",
    )(x2, x2, row(ln_in_g), row(ln_in_b), jnp.swapaxes(w_in, 1, 2), wg_b, row(b_gla_gate[0]), row(gla_norm_g[0]), bias)

    row_spec = lambda width: pl.BlockSpec((ROW_TILE, width), lambda i: (i, 0))
    out = pl.pallas_call(
        functools.partial(_epilogue_kernel, alpha=alpha),
        grid=(n // ROW_TILE,),
        in_specs=[row_spec(d_model), _const_spec((1, d_model)), _const_spec((1, d_model)),
                  row_spec(d_mix), row_spec(d_ple),
                  _const_spec((d_mix, d_model)), _const_spec((d_model, d_model)),
                  _const_spec((1, d_model)), _const_spec((d_ple, d_model)),
                  _const_spec((1, d_model)), _const_spec((1, d_model))],
        out_specs=row_spec(d_model),
        out_shape=jax.ShapeDtypeStruct((n, d_model), x.dtype),
        compiler_params=cparams(dimension_semantics=("parallel",)),
        name="out_proj_ple_norm",
    )(x2, row(ln_in_g), row(ln_in_b), mix, p2, w_out[0].astype(_BF16), w_ple_gate[0].astype(_BF16),
      row(b_ple_gate[0]), w_ple[0].astype(_BF16), row(ln_g[0]), row(ln_b[0]))
    return out.reshape(batch, seq, d_model)
```

```python
import functools
import math

import jax
import jax.numpy as jnp
from jax import lax
from jax.experimental import pallas as pl
from jax.experimental.pallas import tpu as pltpu

CHUNK = 64
N_ATT_HEADS = 8
ATT_HEAD_DIM = 64
D_ATT = N_ATT_HEADS * ATT_HEAD_DIM
LEFT_CHUNKS = 8
LEFT = LEFT_CHUNKS * CHUNK
BAND = LEFT + CHUNK
REL_CLIP = 128
N_GLA_HEADS = 4
GLA_HEAD_K = 64
GLA_HEAD_V = 128
D_GLA_K = N_GLA_HEADS * GLA_HEAD_K
D_GLA_V = N_GLA_HEADS * GLA_HEAD_V
GLA_LOW_RANK = 16
GLA_TAU = 16.0
LN_EPS = 1e-5
RMS_EPS = 1e-6
MASK_VALUE = -1e30
LOG2_E = math.log2(math.e)

LANES = 128
HEAD_PAIR = 2 * ATT_HEAD_DIM
assert HEAD_PAIR == LANES and 2 * GLA_HEAD_K == LANES
N_HEAD_PAIRS = N_ATT_HEADS // 2
PAIR_ROWS = 2 * CHUNK
KEY_SPAN = LEFT + PAIR_ROWS
KEY_BLOCK = LANES
N_SPAN_BLOCKS = KEY_SPAN // KEY_BLOCK
N_LEFT_BLOCKS = LEFT // KEY_BLOCK
assert PAIR_ROWS == KEY_BLOCK
BIAS_TOP = CHUNK
BIAS_BOTTOM = REL_CLIP + PAIR_ROWS
SUM_ROWS = 16
PV_KEY_TILE = 256
N_PV_TILES = -(-KEY_SPAN // PV_KEY_TILE)
assert N_PV_TILES == 3

_OFF_AG = 3 * D_ATT
_OFF_GQK = 4 * D_ATT
_OFF_GV = _OFF_GQK + 2 * D_GLA_K
_OFF_GG = _OFF_GV + D_GLA_V
_OFF_LR = _OFF_GG + D_GLA_V
D_IN_PROJ = _OFF_LR + GLA_LOW_RANK
PROJ_PIECE = 256
N_MAIN_PIECES = _OFF_LR // PROJ_PIECE
N_PROJ_PIECES = N_MAIN_PIECES + 2
PROJ_SLOTS_PER_PAIR = 4

ROW_TILE = 1024
EPILOGUE_SPLIT = 4
SEQ_TILE = 512
VMEM_LIMIT = 56 * 1024 * 1024

_F32 = jnp.float32
_BF16 = jnp.bfloat16
_NT = (((1,), (1,)), ((), ()))
_TN = (((0,), (0,)), ((), ()))


def _layer_norm(xf, g, b):
    mu = jnp.mean(xf, axis=-1, keepdims=True)
    xc = xf - mu
    var = jnp.mean(xc * xc, axis=-1, keepdims=True)
    return xc * lax.rsqrt(var + LN_EPS) * g + b


def _silu(x):
    return x * jax.nn.sigmoid(x)


def _dot(a, b):
    return jnp.dot(a, b, preferred_element_type=_F32)


def _proj_mix_kernel(x0_ref, x_ref, g_ref, b_ref, w32_ref, wg_ref, bg_ref, gn_ref, bias_ref,
                     mix_ref,
                     w_ref, wc_ref, hb_ref, nhb_ref, code_ref, nq_ref, nk_ref, nv_ref, nag_ref, ngqk_ref, ngv_ref, ngg_ref, nla_ref,
                     q_ref, ag_ref, gqk_ref, gv_ref, gg_ref, loga_ref,
                     k_hist, vt_hist, state_ref, *, seq_tile, tiles_per_seq):
    k = pl.program_id(0)
    s = lax.rem(k + tiles_per_seq - 1, tiles_per_seq)
    n_new_blocks = seq_tile // KEY_BLOCK
    next_refs = (nq_ref, nk_ref, nv_ref, nag_ref, ngqk_ref, ngv_ref, ngg_ref, nla_ref)

    @pl.when(k == 0)
    def _():
        for ref in next_refs + (k_hist, vt_hist, state_ref):
            ref[...] = jnp.zeros_like(ref)
        nhb_ref[...] = _layer_norm(x0_ref[...], g_ref[...], b_ref[...]).astype(_BF16)
        for c0 in range(0, _OFF_LR, PROJ_PIECE):
            w_ref[:, c0:c0 + PROJ_PIECE] = w32_ref[c0:c0 + PROJ_PIECE, :].T.astype(_BF16)
        code_t = jnp.concatenate([w32_ref[_OFF_LR:D_IN_PROJ, :],
                                  jnp.zeros((LANES - GLA_LOW_RANK, w32_ref.shape[1]), _F32)], axis=0)
        wc_ref[...] = code_t.T.astype(_BF16)

    @pl.when(s == 0)
    def _():
        k_hist[0:LEFT, :] = jnp.zeros((LEFT, D_ATT), _BF16)
        vt_hist[0:N_LEFT_BLOCKS] = jnp.zeros((N_LEFT_BLOCKS, D_ATT, KEY_BLOCK), _BF16)
        state_ref[...] = jnp.zeros_like(state_ref)

    @pl.when(s > 0)
    def _():
        k_hist[0:LEFT, :] = k_hist[seq_tile:seq_tile + LEFT, :]
        vt_hist[0:N_LEFT_BLOCKS] = vt_hist[n_new_blocks:n_new_blocks + N_LEFT_BLOCKS]

    k_hist[LEFT:LEFT + seq_tile, :] = nk_ref[...]
    for blk in range(n_new_blocks):
        vt_hist[N_LEFT_BLOCKS + blk] = nv_ref[blk * KEY_BLOCK:(blk + 1) * KEY_BLOCK, :].T
    for dst, src in ((q_ref, nq_ref), (ag_ref, nag_ref), (gqk_ref, ngqk_ref), (gv_ref, ngv_ref),
                     (gg_ref, ngg_ref), (loga_ref, nla_ref)):
        dst[...] = src[...]

    hb_ref[...] = nhb_ref[...]

    def norm_next(cp):
        slab = pl.ds(cp * PAIR_ROWS, PAIR_ROWS)
        nhb_ref[slab, :] = _layer_norm(x_ref[slab, :], g_ref[...], b_ref[...]).astype(_BF16)

    def proj_piece(i):
        if i == N_MAIN_PIECES:
            code_ref[...] = _dot(hb_ref[...], wc_ref[...]).astype(_BF16)
            return
        if i == N_MAIN_PIECES + 1:
            logit = _dot(code_ref[...], wg_ref[...]) + bg_ref[...]
            nla_ref[...] = jax.nn.log_sigmoid(logit) * (1.0 / GLA_TAU)
            return
        c0 = i * PROJ_PIECE
        acc = _dot(hb_ref[...], w_ref[:, c0:c0 + PROJ_PIECE])

        def put(ref, base, val):
            ref[:, c0 - base:c0 - base + PROJ_PIECE] = val.astype(ref.dtype)

        if c0 < D_ATT:
            put(nq_ref, 0, acc * (ATT_HEAD_DIM ** -0.5 * LOG2_E))
        elif c0 < 2 * D_ATT:
            put(nk_ref, D_ATT, acc)
        elif c0 < _OFF_AG:
            put(nv_ref, 2 * D_ATT, acc)
        elif c0 < _OFF_GQK:
            put(nag_ref, _OFF_AG, _silu(acc))
        elif c0 < _OFF_GQK + D_GLA_K:
            put(ngqk_ref, _OFF_GQK, acc * (GLA_HEAD_K ** -0.5))
        elif c0 < _OFF_GV:
            put(ngqk_ref, _OFF_GQK, acc)
        elif c0 < _OFF_GG:
            put(ngv_ref, _OFF_GV, acc)
        else:
            put(ngg_ref, _OFF_GG, _silu(acc) * gn_ref[:, c0 - _OFF_GG:c0 - _OFF_GG + PROJ_PIECE])

    lane = lax.broadcasted_iota(jnp.int32, (1, LANES), 1)
    low_half = lane < ATT_HEAD_DIM
    top_rows = lax.broadcasted_iota(jnp.int32, (LANES, 1), 0) < ATT_HEAD_DIM
    row_i = lax.broadcasted_iota(jnp.int32, (CHUNK, CHUNK), 0)
    col_i = lax.broadcasted_iota(jnp.int32, (CHUNK, CHUNK), 1)
    causal = row_i >= col_i
    chunk_row = lax.broadcasted_iota(jnp.int32, (CHUNK, 1), 0)
    ones_rows = jnp.ones((SUM_ROWS, KEY_SPAN), _BF16)

    def pair_body(cp, pieces):
        r0 = cp * PAIR_ROWS
        rows = pl.ds(r0, PAIR_ROWS)
        span = pl.ds(r0, KEY_SPAN)
        first_key = s * seq_tile + cp * PAIR_ROWS - LEFT
        pieces = list(pieces)

        def proj_next():
            piece = pieces.pop(0)
            if piece is not None:
                proj_piece(piece)

        def att_scores(j):
            cols = slice(j * LANES, (j + 1) * LANES)
            qp = q_ref[rows, cols]
            zero = jnp.zeros_like(qp)
            q_blk = jnp.concatenate([jnp.where(low_half, qp, zero), jnp.where(low_half, zero, qp)], axis=0)
            kp = k_hist[span, cols]
            sc = lax.dot_general(kp, q_blk, _NT, preferred_element_type=_F32)
            slabs = [sc[0:BIAS_TOP] + bias_ref[j, 0:BIAS_TOP, :],
                     sc[BIAS_TOP:KEY_SPAN - BIAS_BOTTOM],
                     sc[KEY_SPAN - BIAS_BOTTOM:] + bias_ref[j, BIAS_TOP:, :]]
            sc = jnp.concatenate(slabs, axis=0)
            blocks = [sc[i * KEY_BLOCK:(i + 1) * KEY_BLOCK, :] for i in range(N_SPAN_BLOCKS)]
            for i in range(N_LEFT_BLOCKS - cp):
                blocks[i] = jnp.where(first_key + i * KEY_BLOCK >= 0, blocks[i], MASK_VALUE)
            return jnp.concatenate(blocks, axis=0)

        def att_softmax(sc):
            m = jnp.max(sc, axis=0, keepdims=True)
            return jnp.exp2(sc - m).astype(_BF16)

        def att_out(j, pe):
            cols = slice(j * LANES, (j + 1) * LANES)
            partial = []

            def key_tile(t):
                k0, k1 = t * PV_KEY_TILE, min((t + 1) * PV_KEY_TILE, KEY_SPAN)
                vt = jnp.concatenate([vt_hist[cp + i, cols, :] for i in range(k0 // KEY_BLOCK, k1 // KEY_BLOCK)],
                                     axis=1)
                partial.append(_dot(jnp.concatenate([vt, ones_rows[:, k0:k1]], axis=0), pe[k0:k1, :]))

            def finish():
                ot = functools.reduce(jnp.add, partial)
                inv = 1.0 / ot[LANES:LANES + 1, :]
                num = jnp.where(top_rows, ot[0:LANES, 0:LANES], ot[0:LANES, LANES:2 * LANES])
                scale = jnp.where(top_rows, inv[:, 0:LANES], inv[:, LANES:2 * LANES])
                att = (num * scale).T
                mix_ref[rows, cols] = (att * ag_ref[rows, cols]).astype(_BF16)

            return key_tile, finish

        def gla_cumsum(c):
            crow = pl.ds(cp * PAIR_ROWS + c * CHUNK, CHUNK)
            cum = loga_ref[crow, :]
            shift = 1
            while shift < CHUNK:
                cum = cum + jnp.where(chunk_row >= shift, pltpu.roll(cum, shift, axis=0), 0.0)
                shift *= 2
            return cum

        def gla_scores(c, cum):
            crow = pl.ds(cp * PAIR_ROWS + c * CHUNK, CHUNK)
            cum_end = cum[CHUNK - 1:CHUNK, :]
            e_pos = jnp.exp(cum)
            e_neg = jnp.exp(-cum)
            gq = gqk_ref[crow, 0:D_GLA_K]
            gk = gqk_ref[crow, D_GLA_K:2 * D_GLA_K]
            q_fwd = gq * e_pos
            q_bwd = gq * e_neg
            k_fwd = (gk * e_pos).astype(_BF16)
            k_bwd = (gk * e_neg).astype(_BF16)
            k_end = (gk * jnp.exp(cum_end - cum)).astype(_BF16)
            decay = jnp.exp(cum_end)
            parts = []
            for g in range(N_GLA_HEADS // 2):
                pcols = slice(g * LANES, (g + 1) * LANES)
                prows = slice(g * LANES, (g + 1) * LANES)
                state = state_ref[prows, :]
                zero = jnp.zeros_like(q_fwd[:, pcols])
                stack = lambda a: jnp.concatenate(
                    [jnp.where(low_half, a[:, pcols], zero), jnp.where(low_half, zero, a[:, pcols])],
                    axis=0).astype(_BF16)
                qf, qb = stack(q_fwd), stack(q_bwd)
                a_causal = lax.dot_general(qf, k_bwd[:, pcols], _NT, preferred_element_type=_F32)
                a_anti = lax.dot_general(qb, k_fwd[:, pcols], _NT, preferred_element_type=_F32)
                o_inter = _dot(qf, state.astype(_BF16))
                for e in range(2):
                    head_rows = slice(e * CHUNK, (e + 1) * CHUNK)
                    parts.append((a_causal[head_rows], a_anti[head_rows], o_inter[head_rows]))
                v_pair = gv_ref[crow, 2 * g * GLA_HEAD_V:(2 * g + 2) * GLA_HEAD_V]
                kv = lax.dot_general(k_end[:, pcols], v_pair, _TN, preferred_element_type=_F32)
                new_rows = jnp.where(top_rows, kv[:, 0:GLA_HEAD_V], kv[:, GLA_HEAD_V:])
                decay_rows = jnp.broadcast_to(decay[:, pcols], (LANES, LANES)).T
                state_ref[prows, :] = decay_rows * state + new_rows
            return parts

        def gla_out(c, parts):
            crow = pl.ds(cp * PAIR_ROWS + c * CHUNK, CHUNK)
            for h in range(N_GLA_HEADS):
                a_causal, a_anti, o_inter = parts[h]
                hv = slice(h * GLA_HEAD_V, (h + 1) * GLA_HEAD_V)
                att = jnp.where(causal, a_causal, a_anti).astype(_BF16)
                o = _dot(att, gv_ref[crow, hv]) + o_inter
                o = o * lax.rsqrt(jnp.mean(o * o, axis=-1, keepdims=True) + RMS_EPS)
                mix_ref[crow, D_ATT + h * GLA_HEAD_V:D_ATT + (h + 1) * GLA_HEAD_V] = (
                    o * gg_ref[crow, hv]).astype(_BF16)

        cum0 = gla_cumsum(0)
        cum1 = gla_cumsum(1)
        sc0 = att_scores(0)
        sc1 = att_scores(1)
        g0 = gla_scores(0, cum0)
        pv, finish = att_out(0, att_softmax(sc0))
        pv(0)
        proj_next()
        pv(1)
        sc2 = att_scores(2)
        pv(2)
        finish()
        gla_out(0, g0)
        g1 = gla_scores(1, cum1)
        pv, finish = att_out(1, att_softmax(sc1))
        pv(0)
        proj_next()
        pv(1)
        sc3 = att_scores(3)
        pv(2)
        finish()
        gla_out(1, g1)
        norm_next(cp)
        for j in (2, 3):
            pv, finish = att_out(j, att_softmax(sc2 if j == 2 else sc3))
            pv(0)
            proj_next()
            pv(1)
            pv(2)
            finish()
        assert not pieces

    n_pairs = seq_tile // PAIR_ROWS
    slots = PROJ_SLOTS_PER_PAIR * n_pairs
    heavy = [c // PROJ_PIECE for c in range(_OFF_AG, _OFF_GQK, PROJ_PIECE)] \
        + [c // PROJ_PIECE for c in range(_OFF_GG, _OFF_LR, PROJ_PIECE)] + [N_MAIN_PIECES + 1]
    light = [N_MAIN_PIECES] + [i for i in range(N_MAIN_PIECES) if i not in heavy]
    schedule = []
    for slot in range(slots):
        first = heavy if slot % PROJ_SLOTS_PER_PAIR == 0 else light
        source = first or heavy or light
        schedule.append(source.pop(0) if source else None)
    assert not heavy and not light
    for cp in range(n_pairs):
        pair_body(cp, schedule[cp * PROJ_SLOTS_PER_PAIR:(cp + 1) * PROJ_SLOTS_PER_PAIR])


def _epilogue_kernel(x_ref, g_ref, b_ref, mix_ref, p_ref, wo_ref, wpg_ref, bpg_ref, wp_ref,
                     og_ref, ob_ref, out_ref, *, alpha):
    half = x_ref.shape[0] // EPILOGUE_SPLIT
    halves = [pl.ds(i * half, half) for i in range(EPILOGUE_SPLIT)]

    def residual(rows):
        h = _layer_norm(x_ref[rows, :], g_ref[...], b_ref[...])
        return alpha * h + _dot(mix_ref[rows, :], wo_ref[...])

    def gate_and_embed(rows, r):
        logit = _dot(r.astype(_BF16), wpg_ref[...])
        return logit, _dot(p_ref[rows, :].astype(_BF16), wp_ref[...])

    def finish(rows, r, logit, ple):
        r = r + jax.nn.sigmoid(logit + bpg_ref[...]) * ple
        out_ref[rows, :] = _layer_norm(r, og_ref[...], ob_ref[...])

    rs = [residual(rows) for rows in halves]
    ts = [gate_and_embed(rows, r) for rows, r in zip(halves, rs)]
    for rows, r, t in zip(halves, rs, ts):
        finish(rows, r, *t)


def _const_spec(shape):
    zeros = (0,) * len(shape)
    return pl.BlockSpec(shape, lambda *_: zeros, pipeline_mode=pl.Buffered(1))


def _rel_bias_span(rel_table):
    table = rel_table.astype(_F32)
    heads = table.shape[0]
    near = jnp.flip(table[:, REL_CLIP - (CHUNK - 1):2 * REL_CLIP], axis=1) - table[:, 2 * REL_CLIP:]
    ext = jnp.concatenate([jnp.zeros((heads, BAND - REL_CLIP), _F32), near], axis=1) * LOG2_E
    width = ext.shape[1] + 1
    tiled = jnp.tile(jnp.pad(ext, ((0, 0), (0, 1))), (1, CHUNK))[:, :CHUNK * (width - 1)]
    band = tiled.reshape(heads, CHUNK, width - 1)[:, :, CHUNK - 1:CHUNK - 1 + BAND]
    pad = lambda lo, hi: jnp.pad(band, ((0, 0), (0, 0), (lo, hi)), constant_values=MASK_VALUE)
    both = jnp.stack([pad(0, CHUNK), pad(CHUNK, 0)], axis=1)
    both = both.reshape(N_HEAD_PAIRS, 2, 2, CHUNK, KEY_SPAN)
    full = both.transpose(0, 4, 1, 2, 3).reshape(N_HEAD_PAIRS, KEY_SPAN, 2 * PAIR_ROWS)
    return jnp.concatenate([full[:, :BIAS_TOP], full[:, KEY_SPAN - BIAS_BOTTOM:]], axis=1)


def kernel(x, p, ln_in_g, ln_in_b, w_in, w_gla_gate, b_gla_gate, rel_bias,
           gla_norm_g, w_out, w_ple, w_ple_gate, b_ple_gate, ln_g, ln_b):
    batch, seq, d_model = x.shape
    depth = w_in.shape[0]
    assert depth == 1, "single-layer stack only"
    assert w_in.shape[2] == D_IN_PROJ
    assert seq % SEQ_TILE == 0 and SEQ_TILE % PAIR_ROWS == 0 and SEQ_TILE >= LEFT
    n = batch * seq
    assert n % ROW_TILE == 0
    d_ple = p.shape[-1]
    d_mix = D_ATT + D_GLA_V
    alpha = (2.0 * depth) ** 0.25

    x2 = x.reshape(n, d_model)
    p2 = p.reshape(n, d_ple)
    row = lambda v: v.reshape(1, -1).astype(_F32)
    wg_b = jnp.pad(w_gla_gate[0], ((0, LANES - GLA_LOW_RANK), (0, 0))).astype(_BF16)
    bias = _rel_bias_span(rel_bias[0])

    cparams = functools.partial(pltpu.CompilerParams, vmem_limit_bytes=VMEM_LIMIT)

    tiles_per_seq = seq // SEQ_TILE
    n_tiles = batch * tiles_per_seq
    tile_f32 = lambda width: pltpu.VMEM((SEQ_TILE, width), _F32)
    tile_bf16 = lambda width: pltpu.VMEM((SEQ_TILE, width), _BF16)
    projected = [tile_bf16(D_ATT), tile_f32(D_ATT), tile_f32(2 * D_GLA_K), tile_bf16(D_GLA_V),
                 tile_f32(D_GLA_V), tile_f32(D_GLA_K)]
    mix = pl.pallas_call(
        functools.partial(_proj_mix_kernel, seq_tile=SEQ_TILE, tiles_per_seq=tiles_per_seq),
        grid=(n_tiles + 1,),
        in_specs=[_const_spec((SEQ_TILE, d_model)),
                  pl.BlockSpec((SEQ_TILE, d_model), lambda k: (jnp.minimum(k + 1, n_tiles - 1), 0)),
                  _const_spec((1, d_model)), _const_spec((1, d_model)),
                  pl.BlockSpec((None, D_IN_PROJ, d_model), lambda k: (0, 0, 0), pipeline_mode=pl.Buffered(1)),
                  _const_spec((LANES, D_GLA_K)), _const_spec((1, D_GLA_K)), _const_spec((1, D_GLA_V)),
                  _const_spec((N_HEAD_PAIRS, BIAS_TOP + BIAS_BOTTOM, 2 * PAIR_ROWS))],
        out_specs=pl.BlockSpec((SEQ_TILE, d_mix), lambda k: (jnp.maximum(k - 1, 0), 0)),
        out_shape=jax.ShapeDtypeStruct((n, d_mix), _BF16),
        scratch_shapes=[pltpu.VMEM((d_model, _OFF_LR), _BF16),
                        pltpu.VMEM((d_model, LANES), _BF16),
                        tile_bf16(d_model), tile_bf16(d_model),
                        tile_bf16(LANES),
                        tile_bf16(D_ATT), tile_bf16(D_ATT), tile_bf16(D_ATT),
                        tile_f32(D_ATT), tile_f32(2 * D_GLA_K), tile_bf16(D_GLA_V),
                        tile_f32(D_GLA_V), tile_f32(D_GLA_K)]
                       + projected
                       + [pltpu.VMEM((LEFT + SEQ_TILE, D_ATT), _BF16),
                          pltpu.VMEM(((LEFT + SEQ_TILE) // KEY_BLOCK, D_ATT, KEY_BLOCK), _BF16),
                          pltpu.VMEM((D_GLA_K, GLA_HEAD_V), _F32)],
        compiler_params=cparams(dimension_semantics=("arbitrary",)),
        name="proj_mixers",
    )(x2, x2, row(ln_in_g), row(ln_in_b), jnp.swapaxes(w_in, 1, 2), wg_b, row(b_gla_gate[0]), row(gla_norm_g[0]), bias)

    row_spec = lambda width: pl.BlockSpec((ROW_TILE, width), lambda i: (i, 0))
    out = pl.pallas_call(
        functools.partial(_epilogue_kernel, alpha=alpha),
        grid=(n // ROW_TILE,),
        in_specs=[row_spec(d_model), _const_spec((1, d_model)), _const_spec((1, d_model)),
                  row_spec(d_mix), row_spec(d_ple),
                  _const_spec((d_mix, d_model)), _const_spec((d_model, d_model)),
                  _const_spec((1, d_model)), _const_spec((d_ple, d_model)),
                  _const_spec((1, d_model)), _const_spec((1, d_model))],
        out_specs=row_spec(d_model),
        out_shape=jax.ShapeDtypeStruct((n, d_model), x.dtype),
        compiler_params=cparams(dimension_semantics=("parallel",)),
        name="out_proj_ple_norm",
    )(x2, row(ln_in_g), row(ln_in_b), mix, p2, w_out[0].astype(_BF16), w_ple_gate[0].astype(_BF16),
      row(b_ple_gate[0]), w_ple[0].astype(_BF16), row(ln_g[0]), row(ln_b[0]))
    return out.reshape(batch, seq, d_model)
```

```python
import functools
import math

import jax
import jax.numpy as jnp
from jax import lax
from jax.experimental import pallas as pl
from jax.experimental.pallas import tpu as pltpu

CHUNK = 64
N_ATT_HEADS = 8
ATT_HEAD_DIM = 64
D_ATT = N_ATT_HEADS * ATT_HEAD_DIM
LEFT_CHUNKS = 8
LEFT = LEFT_CHUNKS * CHUNK
BAND = LEFT + CHUNK
REL_CLIP = 128
N_GLA_HEADS = 4
GLA_HEAD_K = 64
GLA_HEAD_V = 128
D_GLA_K = N_GLA_HEADS * GLA_HEAD_K
D_GLA_V = N_GLA_HEADS * GLA_HEAD_V
GLA_LOW_RANK = 16
GLA_TAU = 16.0
LN_EPS = 1e-5
RMS_EPS = 1e-6
MASK_VALUE = -1e30
LOG2_E = math.log2(math.e)

LANES = 128
HEAD_PAIR = 2 * ATT_HEAD_DIM
assert HEAD_PAIR == LANES and 2 * GLA_HEAD_K == LANES
N_HEAD_PAIRS = N_ATT_HEADS // 2
PAIR_ROWS = 2 * CHUNK
KEY_SPAN = LEFT + PAIR_ROWS
KEY_BLOCK = LANES
N_SPAN_BLOCKS = KEY_SPAN // KEY_BLOCK
N_LEFT_BLOCKS = LEFT // KEY_BLOCK
assert PAIR_ROWS == KEY_BLOCK
BIAS_TOP = CHUNK
BIAS_BOTTOM = REL_CLIP + PAIR_ROWS
SUM_ROWS = 16
PV_KEY_TILE = 256
N_PV_TILES = -(-KEY_SPAN // PV_KEY_TILE)
assert N_PV_TILES == 3

_OFF_AG = 3 * D_ATT
_OFF_GQK = 4 * D_ATT
_OFF_GV = _OFF_GQK + 2 * D_GLA_K
_OFF_GG = _OFF_GV + D_GLA_V
_OFF_LR = _OFF_GG + D_GLA_V
D_IN_PROJ = _OFF_LR + GLA_LOW_RANK
PROJ_PIECE = 256
N_MAIN_PIECES = _OFF_LR // PROJ_PIECE
N_PROJ_PIECES = N_MAIN_PIECES + 2
PROJ_SLOTS_PER_PAIR = 4

ROW_TILE = 1024
EPILOGUE_SPLIT = 4
SEQ_TILE = 512
VMEM_LIMIT = 56 * 1024 * 1024

_F32 = jnp.float32
_BF16 = jnp.bfloat16
_NT = (((1,), (1,)), ((), ()))
_TN = (((0,), (0,)), ((), ()))


def _layer_norm(xf, g, b):
    mu = jnp.mean(xf, axis=-1, keepdims=True)
    xc = xf - mu
    var = jnp.mean(xc * xc, axis=-1, keepdims=True)
    return xc * lax.rsqrt(var + LN_EPS) * g + b


def _silu(x):
    return x * jax.nn.sigmoid(x)


def _dot(a, b):
    return jnp.dot(a, b, preferred_element_type=_F32)


def _proj_mix_kernel(x0_ref, x_ref, g_ref, b_ref, w32_ref, wg_ref, bg_ref, gn_ref, bias_ref,
                     mix_ref,
                     w_ref, wc_ref, hb_ref, nhb_ref, code_ref,
                     nq_ref, nk_ref, nv_ref, nag_ref, ngqk_ref, ngv_ref, ngg_ref, nla_ref,
                     q_ref, ag_ref, gqk_ref, gv_ref, gg_ref, loga_ref,
                     k_hist, vt_hist, state_ref, *, seq_tile, tiles_per_seq):
    k = pl.program_id(0)
    s = lax.rem(k + tiles_per_seq - 1, tiles_per_seq)
    n_new_blocks = seq_tile // KEY_BLOCK
    next_refs = (nq_ref, nk_ref, nv_ref, nag_ref, ngqk_ref, ngv_ref, ngg_ref, nla_ref)

    @pl.when(k == 0)
    def _():
        for ref in next_refs + (k_hist, vt_hist, state_ref):
            ref[...] = jnp.zeros_like(ref)
        nhb_ref[...] = _layer_norm(x0_ref[...], g_ref[...], b_ref[...]).astype(_BF16)
        for c0 in range(0, _OFF_LR, PROJ_PIECE):
            w_ref[:, c0:c0 + PROJ_PIECE] = w32_ref[c0:c0 + PROJ_PIECE, :].T.astype(_BF16)
        code_t = jnp.concatenate([w32_ref[_OFF_LR:D_IN_PROJ, :],
                                  jnp.zeros((LANES - GLA_LOW_RANK, w32_ref.shape[1]), _F32)], axis=0)
        wc_ref[...] = code_t.T.astype(_BF16)

    @pl.when(s == 0)
    def _():
        k_hist[0:LEFT, :] = jnp.zeros((LEFT, D_ATT), _BF16)
        vt_hist[0:N_LEFT_BLOCKS] = jnp.zeros((N_LEFT_BLOCKS, D_ATT, KEY_BLOCK), _BF16)
        state_ref[...] = jnp.zeros_like(state_ref)

    @pl.when(s > 0)
    def _():
        k_hist[0:LEFT, :] = k_hist[seq_tile:seq_tile + LEFT, :]
        vt_hist[0:N_LEFT_BLOCKS] = vt_hist[n_new_blocks:n_new_blocks + N_LEFT_BLOCKS]

    k_hist[LEFT:LEFT + seq_tile, :] = nk_ref[...]
    for blk in range(n_new_blocks):
        vt_hist[N_LEFT_BLOCKS + blk] = nv_ref[blk * KEY_BLOCK:(blk + 1) * KEY_BLOCK, :].T
    for dst, src in ((q_ref, nq_ref), (ag_ref, nag_ref), (gqk_ref, ngqk_ref), (gv_ref, ngv_ref),
                     (gg_ref, ngg_ref), (loga_ref, nla_ref)):
        dst[...] = src[...]

    hb_ref[...] = nhb_ref[...]

    def norm_next(cp):
        slab = pl.ds(cp * PAIR_ROWS, PAIR_ROWS)
        nhb_ref[slab, :] = _layer_norm(x_ref[slab, :], g_ref[...], b_ref[...]).astype(_BF16)

    def proj_piece(i):
        if i == N_MAIN_PIECES:
            code_ref[...] = _dot(hb_ref[...], wc_ref[...]).astype(_BF16)
            return
        if i == N_MAIN_PIECES + 1:
            logit = _dot(code_ref[...], wg_ref[...]) + bg_ref[...]
            nla_ref[...] = jax.nn.log_sigmoid(logit) * (1.0 / GLA_TAU)
            return
        c0 = i * PROJ_PIECE
        acc = _dot(hb_ref[...], w_ref[:, c0:c0 + PROJ_PIECE])

        def put(ref, base, val):
            ref[:, c0 - base:c0 - base + PROJ_PIECE] = val.astype(ref.dtype)

        if c0 < D_ATT:
            put(nq_ref, 0, acc * (ATT_HEAD_DIM ** -0.5 * LOG2_E))
        elif c0 < 2 * D_ATT:
            put(nk_ref, D_ATT, acc)
        elif c0 < _OFF_AG:
            put(nv_ref, 2 * D_ATT, acc)
        elif c0 < _OFF_GQK:
            put(nag_ref, _OFF_AG, _silu(acc))
        elif c0 < _OFF_GQK + D_GLA_K:
            put(ngqk_ref, _OFF_GQK, acc * (GLA_HEAD_K ** -0.5))
        elif c0 < _OFF_GV:
            put(ngqk_ref, _OFF_GQK, acc)
        elif c0 < _OFF_GG:
            put(ngv_ref, _OFF_GV, acc)
        else:
            put(ngg_ref, _OFF_GG, _silu(acc) * gn_ref[:, c0 - _OFF_GG:c0 - _OFF_GG + PROJ_PIECE])

    lane = lax.broadcasted_iota(jnp.int32, (1, LANES), 1)
    low_half = lane < ATT_HEAD_DIM
    top_rows = lax.broadcasted_iota(jnp.int32, (LANES, 1), 0) < ATT_HEAD_DIM
    row_i = lax.broadcasted_iota(jnp.int32, (CHUNK, CHUNK), 0)
    col_i = lax.broadcasted_iota(jnp.int32, (CHUNK, CHUNK), 1)
    causal = row_i >= col_i
    chunk_row = lax.broadcasted_iota(jnp.int32, (CHUNK, 1), 0)
    ones_rows = jnp.ones((SUM_ROWS, KEY_SPAN), _BF16)

    def pair_body(cp, pieces):
        r0 = cp * PAIR_ROWS
        rows = pl.ds(r0, PAIR_ROWS)
        span = pl.ds(r0, KEY_SPAN)
        first_key = s * seq_tile + cp * PAIR_ROWS - LEFT
        pieces = list(pieces)

        def proj_next():
            piece = pieces.pop(0)
            if piece is not None:
                proj_piece(piece)

        def att_scores(j):
            cols = slice(j * LANES, (j + 1) * LANES)
            qp = q_ref[rows, cols]
            zero = jnp.zeros_like(qp)
            q_blk = jnp.concatenate([jnp.where(low_half, qp, zero), jnp.where(low_half, zero, qp)], axis=0)
            kp = k_hist[span, cols]
            sc = lax.dot_general(kp, q_blk, _NT, preferred_element_type=_F32)
            slabs = [sc[0:BIAS_TOP] + bias_ref[j, 0:BIAS_TOP, :],
                     sc[BIAS_TOP:KEY_SPAN - BIAS_BOTTOM],
                     sc[KEY_SPAN - BIAS_BOTTOM:] + bias_ref[j, BIAS_TOP:, :]]
            sc = jnp.concatenate(slabs, axis=0)
            blocks = [sc[i * KEY_BLOCK:(i + 1) * KEY_BLOCK, :] for i in range(N_SPAN_BLOCKS)]
            for i in range(N_LEFT_BLOCKS - cp):
                blocks[i] = jnp.where(first_key + i * KEY_BLOCK >= 0, blocks[i], MASK_VALUE)
            return jnp.concatenate(blocks, axis=0)

        def att_softmax(sc):
            m = jnp.max(sc, axis=0, keepdims=True)
            return jnp.exp2(sc - m).astype(_BF16)

        def att_out(j, pe):
            cols = slice(j * LANES, (j + 1) * LANES)
            partial = []

            def key_tile(t):
                k0, k1 = t * PV_KEY_TILE, min((t + 1) * PV_KEY_TILE, KEY_SPAN)
                vt = jnp.concatenate([vt_hist[cp + i, cols, :] for i in range(k0 // KEY_BLOCK, k1 // KEY_BLOCK)],
                                     axis=1)
                partial.append(_dot(jnp.concatenate([vt, ones_rows[:, k0:k1]], axis=0), pe[k0:k1, :]))

            def finish():
                ot = functools.reduce(jnp.add, partial)
                inv = 1.0 / ot[LANES:LANES + 1, :]
                num = jnp.where(top_rows, ot[0:LANES, 0:LANES], ot[0:LANES, LANES:2 * LANES])
                scale = jnp.where(top_rows, inv[:, 0:LANES], inv[:, LANES:2 * LANES])
                att = (num * scale).T
                mix_ref[rows, cols] = (att * ag_ref[rows, cols]).astype(_BF16)

            return key_tile, finish

        def gla_cumsum(c):
            crow = pl.ds(cp * PAIR_ROWS + c * CHUNK, CHUNK)
            cum = loga_ref[crow, :]
            shift = 1
            while shift < CHUNK:
                cum = cum + jnp.where(chunk_row >= shift, pltpu.roll(cum, shift, axis=0), 0.0)
                shift *= 2
            return cum

        def gla_scores(c, cum):
            crow = pl.ds(cp * PAIR_ROWS + c * CHUNK, CHUNK)
            cum_end = cum[CHUNK - 1:CHUNK, :]
            e_pos = jnp.exp(cum)
            e_neg = jnp.exp(-cum)
            gq = gqk_ref[crow, 0:D_GLA_K]
            gk = gqk_ref[crow, D_GLA_K:2 * D_GLA_K]
            q_fwd = gq * e_pos
            q_bwd = gq * e_neg
            k_fwd = (gk * e_pos).astype(_BF16)
            k_bwd = (gk * e_neg).astype(_BF16)
            k_end = (gk * jnp.exp(cum_end - cum)).astype(_BF16)
            decay = jnp.exp(cum_end)
            parts = []
            for g in range(N_GLA_HEADS // 2):
                pcols = slice(g * LANES, (g + 1) * LANES)
                prows = slice(g * LANES, (g + 1) * LANES)
                state = state_ref[prows, :]
                zero = jnp.zeros_like(q_fwd[:, pcols])
                stack = lambda a: jnp.concatenate(
                    [jnp.where(low_half, a[:, pcols], zero), jnp.where(low_half, zero, a[:, pcols])],
                    axis=0).astype(_BF16)
                qf, qb = stack(q_fwd), stack(q_bwd)
                a_causal = lax.dot_general(qf, k_bwd[:, pcols], _NT, preferred_element_type=_F32)
                a_anti = lax.dot_general(qb, k_fwd[:, pcols], _NT, preferred_element_type=_F32)
                o_inter = _dot(qf, state.astype(_BF16))
                for e in range(2):
                    head_rows = slice(e * CHUNK, (e + 1) * CHUNK)
                    parts.append((a_causal[head_rows], a_anti[head_rows], o_inter[head_rows]))
                v_pair = gv_ref[crow, 2 * g * GLA_HEAD_V:(2 * g + 2) * GLA_HEAD_V]
                kv = lax.dot_general(k_end[:, pcols], v_pair, _TN, preferred_element_type=_F32)
                new_rows = jnp.where(top_rows, kv[:, 0:GLA_HEAD_V], kv[:, GLA_HEAD_V:])
                decay_rows = jnp.broadcast_to(decay[:, pcols], (LANES, LANES)).T
                state_ref[prows, :] = decay_rows * state + new_rows
            return parts

        def gla_out(c, parts):
            crow = pl.ds(cp * PAIR_ROWS + c * CHUNK, CHUNK)
            for h in range(N_GLA_HEADS):
                a_causal, a_anti, o_inter = parts[h]
                hv = slice(h * GLA_HEAD_V, (h + 1) * GLA_HEAD_V)
                att = jnp.where(causal, a_causal, a_anti).astype(_BF16)
                o = _dot(att, gv_ref[crow, hv]) + o_inter
                o = o * lax.rsqrt(jnp.mean(o * o, axis=-1, keepdims=True) + RMS_EPS)
                mix_ref[crow, D_ATT + h * GLA_HEAD_V:D_ATT + (h + 1) * GLA_HEAD_V] = (
                    o * gg_ref[crow, hv]).astype(_BF16)

        cum0 = gla_cumsum(0)
        cum1 = gla_cumsum(1)
        sc0 = att_scores(0)
        sc1 = att_scores(1)
        g0 = gla_scores(0, cum0)
        pv, finish = att_out(0, att_softmax(sc0))
        pv(0)
        proj_next()
        pv(1)
        sc2 = att_scores(2)
        pv(2)
        finish()
        gla_out(0, g0)
        g1 = gla_scores(1, cum1)
        pv, finish = att_out(1, att_softmax(sc1))
        pv(0)
        proj_next()
        pv(1)
        sc3 = att_scores(3)
        pv(2)
        finish()
        gla_out(1, g1)
        norm_next(cp)
        for j in (2, 3):
            pv, finish = att_out(j, att_softmax(sc2 if j == 2 else sc3))
            pv(0)
            proj_next()
            pv(1)
            pv(2)
            finish()
        assert not pieces

    n_pairs = seq_tile // PAIR_ROWS
    slots = PROJ_SLOTS_PER_PAIR * n_pairs
    heavy = [c // PROJ_PIECE for c in range(_OFF_AG, _OFF_GQK, PROJ_PIECE)] \
        + [c // PROJ_PIECE for c in range(_OFF_GG, _OFF_LR, PROJ_PIECE)] + [N_MAIN_PIECES + 1]
    light = [N_MAIN_PIECES] + [i for i in range(N_MAIN_PIECES) if i not in heavy]
    schedule = []
    for slot in range(slots):
        first = heavy if slot % PROJ_SLOTS_PER_PAIR == 0 else light
        source = first or heavy or light
        schedule.append(source.pop(0) if source else None)
    assert not heavy and not light
    for cp in range(n_pairs):
        pair_body(cp, schedule[cp * PROJ_SLOTS_PER_PAIR:(cp + 1) * PROJ_SLOTS_PER_PAIR])


def _epilogue_kernel(x_ref, g_ref, b_ref, mix_ref, p_ref, wo32_ref, wpg32_ref, bpg_ref, wp32_ref,
                     og_ref, ob_ref, out_ref, wo_ref, wpg_ref, wp_ref, *, alpha):
    @pl.when(pl.program_id(0) == 0)
    def _():
        for dst, src in ((wo_ref, wo32_ref), (wpg_ref, wpg32_ref), (wp_ref, wp32_ref)):
            dst[...] = src[...].astype(_BF16)

    half = x_ref.shape[0] // EPILOGUE_SPLIT
    halves = [pl.ds(i * half, half) for i in range(EPILOGUE_SPLIT)]

    def residual(rows):
        h = _layer_norm(x_ref[rows, :], g_ref[...], b_ref[...])
        return alpha * h + _dot(mix_ref[rows, :], wo_ref[...])

    def gate_and_embed(rows, r):
        logit = _dot(r.astype(_BF16), wpg_ref[...])
        return logit, _dot(p_ref[rows, :].astype(_BF16), wp_ref[...])

    def finish(rows, r, logit, ple):
        r = r + jax.nn.sigmoid(logit + bpg_ref[...]) * ple
        out_ref[rows, :] = _layer_norm(r, og_ref[...], ob_ref[...])

    rs = [residual(rows) for rows in halves]
    ts = [gate_and_embed(rows, r) for rows, r in zip(halves, rs)]
    for rows, r, t in zip(halves, rs, ts):
        finish(rows, r, *t)


def _const_spec(shape):
    zeros = (0,) * len(shape)
    return pl.BlockSpec(shape, lambda *_: zeros, pipeline_mode=pl.Buffered(1))


def _rel_bias_span(rel_table):
    table = rel_table.astype(_F32)
    heads = table.shape[0]
    near = jnp.flip(table[:, REL_CLIP - (CHUNK - 1):2 * REL_CLIP], axis=1) - table[:, 2 * REL_CLIP:]
    ext = jnp.concatenate([jnp.zeros((heads, BAND - REL_CLIP), _F32), near], axis=1) * LOG2_E
    width = ext.shape[1] + 1
    tiled = jnp.tile(jnp.pad(ext, ((0, 0), (0, 1))), (1, CHUNK))[:, :CHUNK * (width - 1)]
    band = tiled.reshape(heads, CHUNK, width - 1)[:, :, CHUNK - 1:CHUNK - 1 + BAND]
    pad = lambda lo, hi: jnp.pad(band, ((0, 0), (0, 0), (lo, hi)), constant_values=MASK_VALUE)
    both = jnp.stack([pad(0, CHUNK), pad(CHUNK, 0)], axis=1)
    both = both.reshape(N_HEAD_PAIRS, 2, 2, CHUNK, KEY_SPAN)
    full = both.transpose(0, 4, 1, 2, 3).reshape(N_HEAD_PAIRS, KEY_SPAN, 2 * PAIR_ROWS)
    return jnp.concatenate([full[:, :BIAS_TOP], full[:, KEY_SPAN - BIAS_BOTTOM:]], axis=1)


def kernel(x, p, ln_in_g, ln_in_b, w_in, w_gla_gate, b_gla_gate, rel_bias,
           gla_norm_g, w_out, w_ple, w_ple_gate, b_ple_gate, ln_g, ln_b):
    batch, seq, d_model = x.shape
    depth = w_in.shape[0]
    assert depth == 1, "single-layer stack only"
    assert w_in.shape[2] == D_IN_PROJ
    assert seq % SEQ_TILE == 0 and SEQ_TILE % PAIR_ROWS == 0 and SEQ_TILE >= LEFT
    n = batch * seq
    assert n % ROW_TILE == 0
    d_ple = p.shape[-1]
    d_mix = D_ATT + D_GLA_V
    alpha = (2.0 * depth) ** 0.25

    x2 = x.reshape(n, d_model)
    p2 = p.reshape(n, d_ple)
    row = lambda v: v.reshape(1, -1).astype(_F32)
    wg_b = jnp.pad(w_gla_gate[0], ((0, LANES - GLA_LOW_RANK), (0, 0))).astype(_BF16)
    bias = _rel_bias_span(rel_bias[0])

    cparams = functools.partial(pltpu.CompilerParams, vmem_limit_bytes=VMEM_LIMIT)

    tiles_per_seq = seq // SEQ_TILE
    n_tiles = batch * tiles_per_seq
    tile_f32 = lambda width: pltpu.VMEM((SEQ_TILE, width), _F32)
    tile_bf16 = lambda width: pltpu.VMEM((SEQ_TILE, width), _BF16)
    projected = [tile_bf16(D_ATT), tile_f32(D_ATT), tile_f32(2 * D_GLA_K), tile_bf16(D_GLA_V),
                 tile_f32(D_GLA_V), tile_f32(D_GLA_K)]
    mix = pl.pallas_call(
        functools.partial(_proj_mix_kernel, seq_tile=SEQ_TILE, tiles_per_seq=tiles_per_seq),
        grid=(n_tiles + 1,),
        in_specs=[_const_spec((SEQ_TILE, d_model)),
                  pl.BlockSpec((SEQ_TILE, d_model), lambda k: (jnp.minimum(k + 1, n_tiles - 1), 0)),
                  _const_spec((1, d_model)), _const_spec((1, d_model)),
                  pl.BlockSpec((None, D_IN_PROJ, d_model), lambda k: (0, 0, 0), pipeline_mode=pl.Buffered(1)),
                  _const_spec((LANES, D_GLA_K)), _const_spec((1, D_GLA_K)), _const_spec((1, D_GLA_V)),
                  _const_spec((N_HEAD_PAIRS, BIAS_TOP + BIAS_BOTTOM, 2 * PAIR_ROWS))],
        out_specs=pl.BlockSpec((SEQ_TILE, d_mix), lambda k: (jnp.maximum(k - 1, 0), 0)),
        out_shape=jax.ShapeDtypeStruct((n, d_mix), _BF16),
        scratch_shapes=[pltpu.VMEM((d_model, _OFF_LR), _BF16),
                        pltpu.VMEM((d_model, LANES), _BF16),
                        tile_bf16(d_model), tile_bf16(d_model),
                        tile_bf16(LANES),
                        tile_bf16(D_ATT), tile_bf16(D_ATT), tile_bf16(D_ATT),
                        tile_f32(D_ATT), tile_f32(2 * D_GLA_K), tile_bf16(D_GLA_V),
                        tile_f32(D_GLA_V), tile_f32(D_GLA_K)]
                       + projected
                       + [pltpu.VMEM((LEFT + SEQ_TILE, D_ATT), _BF16),
                          pltpu.VMEM(((LEFT + SEQ_TILE) // KEY_BLOCK, D_ATT, KEY_BLOCK), _BF16),
                          pltpu.VMEM((D_GLA_K, GLA_HEAD_V), _F32)],
        compiler_params=cparams(dimension_semantics=("arbitrary",)),
        name="proj_mixers",
    )(x2, x2, row(ln_in_g), row(ln_in_b), jnp.swapaxes(w_in, 1, 2), wg_b, row(b_gla_gate[0]), row(gla_norm_g[0]), bias)

    row_spec = lambda width: pl.BlockSpec((ROW_TILE, width), lambda i: (i, 0))
    layer_weight = lambda rows, cols: pl.BlockSpec((None, rows, cols), lambda i: (0, 0, 0),
                                                   pipeline_mode=pl.Buffered(1))
    out = pl.pallas_call(
        functools.partial(_epilogue_kernel, alpha=alpha),
        grid=(n // ROW_TILE,),
        in_specs=[row_spec(d_model), _const_spec((1, d_model)), _const_spec((1, d_model)),
                  row_spec(d_mix), row_spec(d_ple),
                  layer_weight(d_mix, d_model), layer_weight(d_model, d_model),
                  _const_spec((1, d_model)), layer_weight(d_ple, d_model),
                  _const_spec((1, d_model)), _const_spec((1, d_model))],
        out_specs=row_spec(d_model),
        out_shape=jax.ShapeDtypeStruct((n, d_model), x.dtype),
        scratch_shapes=[pltpu.VMEM((d_mix, d_model), _BF16), pltpu.VMEM((d_model, d_model), _BF16),
                        pltpu.VMEM((d_ple, d_model), _BF16)],
        compiler_params=cparams(dimension_semantics=("arbitrary",)),
        name="out_proj_ple_norm",
    )(x2, row(ln_in_g), row(ln_in_b), mix, p2, w_out, w_ple_gate,
      row(b_ple_gate[0]), w_ple, row(ln_g[0]), row(ln_b[0]))
    return out.reshape(batch, seq, d_model)
```

```python
import functools
import math

import jax
import jax.numpy as jnp
from jax import lax
from jax.experimental import pallas as pl
from jax.experimental.pallas import tpu as pltpu

CHUNK = 64
N_ATT_HEADS = 8
ATT_HEAD_DIM = 64
D_ATT = N_ATT_HEADS * ATT_HEAD_DIM
LEFT_CHUNKS = 8
LEFT = LEFT_CHUNKS * CHUNK
BAND = LEFT + CHUNK
REL_CLIP = 128
N_GLA_HEADS = 4
GLA_HEAD_K = 64
GLA_HEAD_V = 128
D_GLA_K = N_GLA_HEADS * GLA_HEAD_K
D_GLA_V = N_GLA_HEADS * GLA_HEAD_V
GLA_LOW_RANK = 16
GLA_TAU = 16.0
LN_EPS = 1e-5
RMS_EPS = 1e-6
MASK_VALUE = -1e30
LOG2_E = math.log2(math.e)

LANES = 128
HEAD_PAIR = 2 * ATT_HEAD_DIM
assert HEAD_PAIR == LANES and 2 * GLA_HEAD_K == LANES
N_HEAD_PAIRS = N_ATT_HEADS // 2
PAIR_ROWS = 2 * CHUNK
KEY_SPAN = LEFT + PAIR_ROWS
KEY_BLOCK = LANES
N_SPAN_BLOCKS = KEY_SPAN // KEY_BLOCK
N_LEFT_BLOCKS = LEFT // KEY_BLOCK
assert PAIR_ROWS == KEY_BLOCK
BIAS_TOP = CHUNK
BIAS_BOTTOM = REL_CLIP + PAIR_ROWS
SUM_ROWS = 16
PV_KEY_TILE = 256
N_PV_TILES = -(-KEY_SPAN // PV_KEY_TILE)
assert N_PV_TILES == 3

_OFF_AG = 3 * D_ATT
_OFF_GQK = 4 * D_ATT
_OFF_GV = _OFF_GQK + 2 * D_GLA_K
_OFF_GG = _OFF_GV + D_GLA_V
_OFF_LR = _OFF_GG + D_GLA_V
D_IN_PROJ = _OFF_LR + GLA_LOW_RANK
PROJ_PIECE = 256
N_MAIN_PIECES = _OFF_LR // PROJ_PIECE
N_PROJ_PIECES = N_MAIN_PIECES + 2
PROJ_SLOTS_PER_PAIR = 4

ROW_TILE = 1024
EPILOGUE_SPLIT = 4
SEQ_TILE = 512
VMEM_LIMIT = 56 * 1024 * 1024

_F32 = jnp.float32
_BF16 = jnp.bfloat16
_NT = (((1,), (1,)), ((), ()))
_TN = (((0,), (0,)), ((), ()))


def _layer_norm(xf, g, b):
    mu = jnp.mean(xf, axis=-1, keepdims=True)
    xc = xf - mu
    var = jnp.mean(xc * xc, axis=-1, keepdims=True)
    return xc * lax.rsqrt(var + LN_EPS) * g + b


def _silu(x):
    return x * jax.nn.sigmoid(x)


def _dot(a, b):
    return jnp.dot(a, b, preferred_element_type=_F32)


def _proj_mix_kernel(x0_ref, x_ref, g_ref, b_ref, w32_ref, wg_ref, bg_ref, gn_ref, bias_ref,
                     mix_ref,
                     w_ref, wc_ref, hb_ref, nhb_ref, code_ref,
                     nq_ref, nk_ref, nv_ref, nag_ref, ngqk_ref, ngv_ref, ngg_ref, nla_ref,
                     q_ref, ag_ref, gqk_ref, gv_ref, gg_ref, loga_ref,
                     k_hist, vt_hist, state_ref, *, seq_tile, tiles_per_seq):
    k = pl.program_id(0)
    s = lax.rem(k + tiles_per_seq - 1, tiles_per_seq)
    n_new_blocks = seq_tile // KEY_BLOCK
    next_refs = (nq_ref, nk_ref, nv_ref, nag_ref, ngqk_ref, ngv_ref, ngg_ref, nla_ref)

    @pl.when(k == 0)
    def _():
        for ref in next_refs + (k_hist, vt_hist, state_ref):
            ref[...] = jnp.zeros_like(ref)
        nhb_ref[...] = _layer_norm(x0_ref[...], g_ref[...], b_ref[...]).astype(_BF16)
        for c0 in range(0, _OFF_LR, PROJ_PIECE):
            w_ref[:, c0:c0 + PROJ_PIECE] = w32_ref[c0:c0 + PROJ_PIECE, :].T.astype(_BF16)
        code_t = jnp.concatenate([w32_ref[_OFF_LR:D_IN_PROJ, :],
                                  jnp.zeros((LANES - GLA_LOW_RANK, w32_ref.shape[1]), _F32)], axis=0)
        wc_ref[...] = code_t.T.astype(_BF16)

    def start_tile(first_tile):
        if first_tile:
            k_hist[0:LEFT, :] = jnp.zeros((LEFT, D_ATT), _BF16)
            vt_hist[0:N_LEFT_BLOCKS] = jnp.zeros((N_LEFT_BLOCKS, D_ATT, KEY_BLOCK), _BF16)
            state_ref[...] = jnp.zeros_like(state_ref)
        else:
            k_hist[0:LEFT, :] = k_hist[seq_tile:seq_tile + LEFT, :]
            vt_hist[0:N_LEFT_BLOCKS] = vt_hist[n_new_blocks:n_new_blocks + N_LEFT_BLOCKS]
        k_hist[LEFT:LEFT + seq_tile, :] = nk_ref[...]
        for blk in range(n_new_blocks):
            vt_hist[N_LEFT_BLOCKS + blk] = nv_ref[blk * KEY_BLOCK:(blk + 1) * KEY_BLOCK, :].T
        for dst, src in ((q_ref, nq_ref), (ag_ref, nag_ref), (gqk_ref, ngqk_ref), (gv_ref, ngv_ref),
                         (gg_ref, ngg_ref), (loga_ref, nla_ref)):
            dst[...] = src[...]
        hb_ref[...] = nhb_ref[...]

    def norm_next(cp):
        slab = pl.ds(cp * PAIR_ROWS, PAIR_ROWS)
        nhb_ref[slab, :] = _layer_norm(x_ref[slab, :], g_ref[...], b_ref[...]).astype(_BF16)

    def proj_piece(i):
        if i == N_MAIN_PIECES:
            code_ref[...] = _dot(hb_ref[...], wc_ref[...]).astype(_BF16)
            return
        if i == N_MAIN_PIECES + 1:
            logit = _dot(code_ref[...], wg_ref[...]) + bg_ref[...]
            nla_ref[...] = jax.nn.log_sigmoid(logit) * (1.0 / GLA_TAU)
            return
        c0 = i * PROJ_PIECE
        acc = _dot(hb_ref[...], w_ref[:, c0:c0 + PROJ_PIECE])

        def put(ref, base, val):
            ref[:, c0 - base:c0 - base + PROJ_PIECE] = val.astype(ref.dtype)

        if c0 < D_ATT:
            put(nq_ref, 0, acc * (ATT_HEAD_DIM ** -0.5 * LOG2_E))
        elif c0 < 2 * D_ATT:
            put(nk_ref, D_ATT, acc)
        elif c0 < _OFF_AG:
            put(nv_ref, 2 * D_ATT, acc)
        elif c0 < _OFF_GQK:
            put(nag_ref, _OFF_AG, _silu(acc))
        elif c0 < _OFF_GQK + D_GLA_K:
            put(ngqk_ref, _OFF_GQK, acc * (GLA_HEAD_K ** -0.5))
        elif c0 < _OFF_GV:
            put(ngqk_ref, _OFF_GQK, acc)
        elif c0 < _OFF_GG:
            put(ngv_ref, _OFF_GV, acc)
        else:
            put(ngg_ref, _OFF_GG, _silu(acc) * gn_ref[:, c0 - _OFF_GG:c0 - _OFF_GG + PROJ_PIECE])

    lane = lax.broadcasted_iota(jnp.int32, (1, LANES), 1)
    low_half = lane < ATT_HEAD_DIM
    top_rows = lax.broadcasted_iota(jnp.int32, (LANES, 1), 0) < ATT_HEAD_DIM
    row_i = lax.broadcasted_iota(jnp.int32, (CHUNK, CHUNK), 0)
    col_i = lax.broadcasted_iota(jnp.int32, (CHUNK, CHUNK), 1)
    causal = row_i >= col_i
    chunk_row = lax.broadcasted_iota(jnp.int32, (CHUNK, 1), 0)
    ones_rows = jnp.ones((SUM_ROWS, KEY_SPAN), _BF16)

    def pair_body(cp, pieces, first_tile):
        r0 = cp * PAIR_ROWS
        rows = pl.ds(r0, PAIR_ROWS)
        skip = (N_LEFT_BLOCKS - cp) * KEY_BLOCK if first_tile else 0
        n_keys = KEY_SPAN - skip
        span = pl.ds(r0 + skip, n_keys)
        pieces = list(pieces)

        def proj_next():
            piece = pieces.pop(0)
            if piece is not None:
                proj_piece(piece)

        def att_scores(j):
            cols = slice(j * LANES, (j + 1) * LANES)
            qp = q_ref[rows, cols]
            zero = jnp.zeros_like(qp)
            q_blk = jnp.concatenate([jnp.where(low_half, qp, zero), jnp.where(low_half, zero, qp)], axis=0)
            kp = k_hist[span, cols]
            sc = lax.dot_general(kp, q_blk, _NT, preferred_element_type=_F32)
            slabs = []
            for lo, hi, bias_row in ((0, BIAS_TOP, 0), (BIAS_TOP, KEY_SPAN - BIAS_BOTTOM, None),
                                     (KEY_SPAN - BIAS_BOTTOM, KEY_SPAN, BIAS_TOP)):
                start = max(lo, skip)
                if start >= hi:
                    continue
                slab = sc[start - skip:hi - skip]
                if bias_row is not None:
                    slab = slab + bias_ref[j, bias_row + start - lo:bias_row + hi - lo, :]
                slabs.append(slab)
            return jnp.concatenate(slabs, axis=0)

        def att_softmax(sc):
            m = jnp.max(sc, axis=0, keepdims=True)
            return jnp.exp2(sc - m).astype(_BF16)

        def att_out(j, pe):
            cols = slice(j * LANES, (j + 1) * LANES)
            partial = []

            def key_tile(t):
                k0, k1 = t * PV_KEY_TILE, min((t + 1) * PV_KEY_TILE, n_keys)
                if k0 >= k1:
                    return
                first_block = cp + (skip + k0) // KEY_BLOCK
                vt = jnp.concatenate([vt_hist[first_block + i, cols, :] for i in range((k1 - k0) // KEY_BLOCK)],
                                     axis=1)
                partial.append(_dot(jnp.concatenate([vt, ones_rows[:, k0:k1]], axis=0), pe[k0:k1, :]))

            def finish():
                ot = functools.reduce(jnp.add, partial)
                inv = 1.0 / ot[LANES:LANES + 1, :]
                num = jnp.where(top_rows, ot[0:LANES, 0:LANES], ot[0:LANES, LANES:2 * LANES])
                scale = jnp.where(top_rows, inv[:, 0:LANES], inv[:, LANES:2 * LANES])
                att = (num * scale).T
                mix_ref[rows, cols] = (att * ag_ref[rows, cols]).astype(_BF16)

            return key_tile, finish

        def gla_cumsum(c):
            crow = pl.ds(cp * PAIR_ROWS + c * CHUNK, CHUNK)
            cum = loga_ref[crow, :]
            shift = 1
            while shift < CHUNK:
                cum = cum + jnp.where(chunk_row >= shift, pltpu.roll(cum, shift, axis=0), 0.0)
                shift *= 2
            return cum

        def gla_scores(c, cum):
            crow = pl.ds(cp * PAIR_ROWS + c * CHUNK, CHUNK)
            cum_end = cum[CHUNK - 1:CHUNK, :]
            e_pos = jnp.exp(cum)
            e_neg = jnp.exp(-cum)
            gq = gqk_ref[crow, 0:D_GLA_K]
            gk = gqk_ref[crow, D_GLA_K:2 * D_GLA_K]
            q_fwd = gq * e_pos
            q_bwd = gq * e_neg
            k_fwd = (gk * e_pos).astype(_BF16)
            k_bwd = (gk * e_neg).astype(_BF16)
            k_end = (gk * jnp.exp(cum_end - cum)).astype(_BF16)
            decay = jnp.exp(cum_end)
            parts = []
            for g in range(N_GLA_HEADS // 2):
                pcols = slice(g * LANES, (g + 1) * LANES)
                prows = slice(g * LANES, (g + 1) * LANES)
                state = state_ref[prows, :]
                zero = jnp.zeros_like(q_fwd[:, pcols])
                stack = lambda a: jnp.concatenate(
                    [jnp.where(low_half, a[:, pcols], zero), jnp.where(low_half, zero, a[:, pcols])],
                    axis=0).astype(_BF16)
                qf, qb = stack(q_fwd), stack(q_bwd)
                a_causal = lax.dot_general(qf, k_bwd[:, pcols], _NT, preferred_element_type=_F32)
                a_anti = lax.dot_general(qb, k_fwd[:, pcols], _NT, preferred_element_type=_F32)
                o_inter = _dot(qf, state.astype(_BF16))
                for e in range(2):
                    head_rows = slice(e * CHUNK, (e + 1) * CHUNK)
                    parts.append((a_causal[head_rows], a_anti[head_rows], o_inter[head_rows]))
                v_pair = gv_ref[crow, 2 * g * GLA_HEAD_V:(2 * g + 2) * GLA_HEAD_V]
                kv = lax.dot_general(k_end[:, pcols], v_pair, _TN, preferred_element_type=_F32)
                new_rows = jnp.where(top_rows, kv[:, 0:GLA_HEAD_V], kv[:, GLA_HEAD_V:])
                decay_rows = jnp.broadcast_to(decay[:, pcols], (LANES, LANES)).T
                state_ref[prows, :] = decay_rows * state + new_rows
            return parts

        def gla_out(c, parts):
            crow = pl.ds(cp * PAIR_ROWS + c * CHUNK, CHUNK)
            for h in range(N_GLA_HEADS):
                a_causal, a_anti, o_inter = parts[h]
                hv = slice(h * GLA_HEAD_V, (h + 1) * GLA_HEAD_V)
                att = jnp.where(causal, a_causal, a_anti).astype(_BF16)
                o = _dot(att, gv_ref[crow, hv]) + o_inter
                o = o * lax.rsqrt(jnp.mean(o * o, axis=-1, keepdims=True) + RMS_EPS)
                mix_ref[crow, D_ATT + h * GLA_HEAD_V:D_ATT + (h + 1) * GLA_HEAD_V] = (
                    o * gg_ref[crow, hv]).astype(_BF16)

        cum0 = gla_cumsum(0)
        cum1 = gla_cumsum(1)
        sc0 = att_scores(0)
        sc1 = att_scores(1)
        g0 = gla_scores(0, cum0)
        pv, finish = att_out(0, att_softmax(sc0))
        pv(0)
        proj_next()
        pv(1)
        sc2 = att_scores(2)
        pv(2)
        finish()
        gla_out(0, g0)
        g1 = gla_scores(1, cum1)
        pv, finish = att_out(1, att_softmax(sc1))
        pv(0)
        proj_next()
        pv(1)
        sc3 = att_scores(3)
        pv(2)
        finish()
        gla_out(1, g1)
        norm_next(cp)
        for j in (2, 3):
            pv, finish = att_out(j, att_softmax(sc2 if j == 2 else sc3))
            pv(0)
            proj_next()
            pv(1)
            pv(2)
            finish()
        assert not pieces

    n_pairs = seq_tile // PAIR_ROWS
    slots = PROJ_SLOTS_PER_PAIR * n_pairs
    heavy = [c // PROJ_PIECE for c in range(_OFF_AG, _OFF_GQK, PROJ_PIECE)] \
        + [c // PROJ_PIECE for c in range(_OFF_GG, _OFF_LR, PROJ_PIECE)] + [N_MAIN_PIECES + 1]
    light = [N_MAIN_PIECES] + [i for i in range(N_MAIN_PIECES) if i not in heavy]
    schedule = []
    for slot in range(slots):
        first = heavy if slot % PROJ_SLOTS_PER_PAIR == 0 else light
        source = first or heavy or light
        schedule.append(source.pop(0) if source else None)
    assert not heavy and not light

    def tile_body(first_tile):
        start_tile(first_tile)
        for cp in range(n_pairs):
            pair_body(cp, schedule[cp * PROJ_SLOTS_PER_PAIR:(cp + 1) * PROJ_SLOTS_PER_PAIR], first_tile)

    pl.when(s == 0)(functools.partial(tile_body, True))
    pl.when(s > 0)(functools.partial(tile_body, False))


def _epilogue_kernel(x_ref, g_ref, b_ref, mix_ref, p_ref, wo32_ref, wpg32_ref, bpg_ref, wp32_ref,
                     og_ref, ob_ref, out_ref, wo_ref, wpg_ref, wp_ref, *, alpha):
    @pl.when(pl.program_id(0) == 0)
    def _():
        for dst, src in ((wo_ref, wo32_ref), (wpg_ref, wpg32_ref), (wp_ref, wp32_ref)):
            dst[...] = src[...].astype(_BF16)

    half = x_ref.shape[0] // EPILOGUE_SPLIT
    halves = [pl.ds(i * half, half) for i in range(EPILOGUE_SPLIT)]

    def residual(rows):
        h = _layer_norm(x_ref[rows, :], g_ref[...], b_ref[...])
        return alpha * h + _dot(mix_ref[rows, :], wo_ref[...])

    def gate_and_embed(rows, r):
        logit = _dot(r.astype(_BF16), wpg_ref[...])
        return logit, _dot(p_ref[rows, :].astype(_BF16), wp_ref[...])

    def finish(rows, r, logit, ple):
        r = r + jax.nn.sigmoid(logit + bpg_ref[...]) * ple
        out_ref[rows, :] = _layer_norm(r, og_ref[...], ob_ref[...])

    rs = [residual(rows) for rows in halves]
    ts = [gate_and_embed(rows, r) for rows, r in zip(halves, rs)]
    for rows, r, t in zip(halves, rs, ts):
        finish(rows, r, *t)


def _const_spec(shape):
    zeros = (0,) * len(shape)
    return pl.BlockSpec(shape, lambda *_: zeros, pipeline_mode=pl.Buffered(1))


def _rel_bias_span(rel_table):
    table = rel_table.astype(_F32)
    heads = table.shape[0]
    near = jnp.flip(table[:, REL_CLIP - (CHUNK - 1):2 * REL_CLIP], axis=1) - table[:, 2 * REL_CLIP:]
    ext = jnp.concatenate([jnp.zeros((heads, BAND - REL_CLIP), _F32), near], axis=1) * LOG2_E
    width = ext.shape[1] + 1
    tiled = jnp.tile(jnp.pad(ext, ((0, 0), (0, 1))), (1, CHUNK))[:, :CHUNK * (width - 1)]
    band = tiled.reshape(heads, CHUNK, width - 1)[:, :, CHUNK - 1:CHUNK - 1 + BAND]
    pad = lambda lo, hi: jnp.pad(band, ((0, 0), (0, 0), (lo, hi)), constant_values=MASK_VALUE)
    both = jnp.stack([pad(0, CHUNK), pad(CHUNK, 0)], axis=1)
    both = both.reshape(N_HEAD_PAIRS, 2, 2, CHUNK, KEY_SPAN)
    full = both.transpose(0, 4, 1, 2, 3).reshape(N_HEAD_PAIRS, KEY_SPAN, 2 * PAIR_ROWS)
    return jnp.concatenate([full[:, :BIAS_TOP], full[:, KEY_SPAN - BIAS_BOTTOM:]], axis=1)


def kernel(x, p, ln_in_g, ln_in_b, w_in, w_gla_gate, b_gla_gate, rel_bias,
           gla_norm_g, w_out, w_ple, w_ple_gate, b_ple_gate, ln_g, ln_b):
    batch, seq, d_model = x.shape
    depth = w_in.shape[0]
    assert depth == 1, "single-layer stack only"
    assert w_in.shape[2] == D_IN_PROJ
    assert seq % SEQ_TILE == 0 and SEQ_TILE % PAIR_ROWS == 0 and SEQ_TILE >= LEFT
    n = batch * seq
    assert n % ROW_TILE == 0
    d_ple = p.shape[-1]
    d_mix = D_ATT + D_GLA_V
    alpha = (2.0 * depth) ** 0.25

    x2 = x.reshape(n, d_model)
    p2 = p.reshape(n, d_ple)
    row = lambda v: v.reshape(1, -1).astype(_F32)
    wg_b = jnp.pad(w_gla_gate[0], ((0, LANES - GLA_LOW_RANK), (0, 0))).astype(_BF16)
    bias = _rel_bias_span(rel_bias[0])

    cparams = functools.partial(pltpu.CompilerParams, vmem_limit_bytes=VMEM_LIMIT)

    tiles_per_seq = seq // SEQ_TILE
    n_tiles = batch * tiles_per_seq
    tile_f32 = lambda width: pltpu.VMEM((SEQ_TILE, width), _F32)
    tile_bf16 = lambda width: pltpu.VMEM((SEQ_TILE, width), _BF16)
    projected = [tile_bf16(D_ATT), tile_f32(D_ATT), tile_f32(2 * D_GLA_K), tile_bf16(D_GLA_V),
                 tile_f32(D_GLA_V), tile_f32(D_GLA_K)]
    mix = pl.pallas_call(
        functools.partial(_proj_mix_kernel, seq_tile=SEQ_TILE, tiles_per_seq=tiles_per_seq),
        grid=(n_tiles + 1,),
        in_specs=[_const_spec((SEQ_TILE, d_model)),
                  pl.BlockSpec((SEQ_TILE, d_model), lambda k: (jnp.minimum(k + 1, n_tiles - 1), 0)),
                  _const_spec((1, d_model)), _const_spec((1, d_model)),
                  pl.BlockSpec((None, D_IN_PROJ, d_model), lambda k: (0, 0, 0), pipeline_mode=pl.Buffered(1)),
                  _const_spec((LANES, D_GLA_K)), _const_spec((1, D_GLA_K)), _const_spec((1, D_GLA_V)),
                  _const_spec((N_HEAD_PAIRS, BIAS_TOP + BIAS_BOTTOM, 2 * PAIR_ROWS))],
        out_specs=pl.BlockSpec((SEQ_TILE, d_mix), lambda k: (jnp.maximum(k - 1, 0), 0)),
        out_shape=jax.ShapeDtypeStruct((n, d_mix), _BF16),
        scratch_shapes=[pltpu.VMEM((d_model, _OFF_LR), _BF16),
                        pltpu.VMEM((d_model, LANES), _BF16),
                        tile_bf16(d_model), tile_bf16(d_model),
                        tile_bf16(LANES),
                        tile_bf16(D_ATT), tile_bf16(D_ATT), tile_bf16(D_ATT),
                        tile_f32(D_ATT), tile_f32(2 * D_GLA_K), tile_bf16(D_GLA_V),
                        tile_f32(D_GLA_V), tile_f32(D_GLA_K)]
                       + projected
                       + [pltpu.VMEM((LEFT + SEQ_TILE, D_ATT), _BF16),
                          pltpu.VMEM(((LEFT + SEQ_TILE) // KEY_BLOCK, D_ATT, KEY_BLOCK), _BF16),
                          pltpu.VMEM((D_GLA_K, GLA_HEAD_V), _F32)],
        compiler_params=cparams(dimension_semantics=("arbitrary",)),
        name="proj_mixers",
    )(x2, x2, row(ln_in_g), row(ln_in_b), jnp.swapaxes(w_in, 1, 2), wg_b, row(b_gla_gate[0]), row(gla_norm_g[0]), bias)

    row_spec = lambda width: pl.BlockSpec((ROW_TILE, width), lambda i: (i, 0))
    layer_weight = lambda rows, cols: pl.BlockSpec((None, rows, cols), lambda i: (0, 0, 0),
                                                   pipeline_mode=pl.Buffered(1))
    out = pl.pallas_call(
        functools.partial(_epilogue_kernel, alpha=alpha),
        grid=(n // ROW_TILE,),
        in_specs=[row_spec(d_model), _const_spec((1, d_model)), _const_spec((1, d_model)),
                  row_spec(d_mix), row_spec(d_ple),
                  layer_weight(d_mix, d_model), layer_weight(d_model, d_model),
                  _const_spec((1, d_model)), layer_weight(d_ple, d_model),
                  _const_spec((1, d_model)), _const_spec((1, d_model))],
        out_specs=row_spec(d_model),
        out_shape=jax.ShapeDtypeStruct((n, d_model), x.dtype),
        scratch_shapes=[pltpu.VMEM((d_mix, d_model), _BF16), pltpu.VMEM((d_model, d_model), _BF16),
                        pltpu.VMEM((d_ple, d_model), _BF16)],
        compiler_params=cparams(dimension_semantics=("arbitrary",)),
        name="out_proj_ple_norm",
    )(x2, row(ln_in_g), row(ln_in_b), mix, p2, w_out, w_ple_gate,
      row(b_ple_gate[0]), w_ple, row(ln_g[0]), row(ln_b[0]))
    return out.reshape(batch, seq, d_model)
```

```python
import functools
import math

import jax
import jax.numpy as jnp
from jax import lax
from jax.experimental import pallas as pl
from jax.experimental.pallas import tpu as pltpu

CHUNK = 64
N_ATT_HEADS = 8
ATT_HEAD_DIM = 64
D_ATT = N_ATT_HEADS * ATT_HEAD_DIM
LEFT_CHUNKS = 8
LEFT = LEFT_CHUNKS * CHUNK
BAND = LEFT + CHUNK
REL_CLIP = 128
N_GLA_HEADS = 4
GLA_HEAD_K = 64
GLA_HEAD_V = 128
D_GLA_K = N_GLA_HEADS * GLA_HEAD_K
D_GLA_V = N_GLA_HEADS * GLA_HEAD_V
GLA_LOW_RANK = 16
GLA_TAU = 16.0
LN_EPS = 1e-5
RMS_EPS = 1e-6
MASK_VALUE = -1e30
LOG2_E = math.log2(math.e)

LANES = 128
HEAD_PAIR = 2 * ATT_HEAD_DIM
assert HEAD_PAIR == LANES and 2 * GLA_HEAD_K == LANES
N_HEAD_PAIRS = N_ATT_HEADS // 2
PAIR_ROWS = 2 * CHUNK
KEY_SPAN = LEFT + PAIR_ROWS
KEY_BLOCK = LANES
N_SPAN_BLOCKS = KEY_SPAN // KEY_BLOCK
N_LEFT_BLOCKS = LEFT // KEY_BLOCK
assert PAIR_ROWS == KEY_BLOCK
BIAS_TOP = CHUNK
BIAS_BOTTOM = REL_CLIP + PAIR_ROWS
SUM_ROWS = 16
PV_KEY_TILE = 256
N_PV_TILES = -(-KEY_SPAN // PV_KEY_TILE)
assert N_PV_TILES == 3

_OFF_AG = 3 * D_ATT
_OFF_GQK = 4 * D_ATT
_OFF_GV = _OFF_GQK + 2 * D_GLA_K
_OFF_GG = _OFF_GV + D_GLA_V
_OFF_LR = _OFF_GG + D_GLA_V
D_IN_PROJ = _OFF_LR + GLA_LOW_RANK
PROJ_PIECE = 256
N_MAIN_PIECES = _OFF_LR // PROJ_PIECE
N_PROJ_PIECES = N_MAIN_PIECES + 2
PROJ_SLOTS_PER_PAIR = 4

ROW_TILE = 1024
EPILOGUE_SPLIT = 4
SEQ_TILE = 512
VMEM_LIMIT = 56 * 1024 * 1024

_F32 = jnp.float32
_BF16 = jnp.bfloat16
_NT = (((1,), (1,)), ((), ()))
_TN = (((0,), (0,)), ((), ()))


def _layer_norm(xf, g, b):
    mu = jnp.mean(xf, axis=-1, keepdims=True)
    xc = xf - mu
    var = jnp.mean(xc * xc, axis=-1, keepdims=True)
    return xc * lax.rsqrt(var + LN_EPS) * g + b


def _silu(x):
    return x * jax.nn.sigmoid(x)


def _dot(a, b):
    return jnp.dot(a, b, preferred_element_type=_F32)


def _proj_mix_kernel(x0_ref, x_ref, g_ref, b_ref, w32_ref, wg_ref, bg_ref, gn_ref, bias_ref,
                     mix_ref,
                     w_ref, wc_ref, hb_ref, nhb_ref, code_ref,
                     nq_ref, nk_ref, nv_ref, nag_ref, ngqk_ref, ngv_ref, ngg_ref, nla_ref,
                     q_ref, ag_ref, gqk_ref, gv_ref, gg_ref, loga_ref,
                     k_hist, vt_hist, state_ref, *, seq_tile, tiles_per_seq):
    k = pl.program_id(0)
    s = lax.rem(k + tiles_per_seq - 1, tiles_per_seq)
    n_new_blocks = seq_tile // KEY_BLOCK
    next_refs = (nq_ref, nk_ref, nv_ref, nag_ref, ngqk_ref, ngv_ref, ngg_ref, nla_ref)

    @pl.when(k == 0)
    def _():
        for ref in next_refs + (k_hist, vt_hist, state_ref):
            ref[...] = jnp.zeros_like(ref)
        nhb_ref[...] = _layer_norm(x0_ref[...], g_ref[...], b_ref[...]).astype(_BF16)
        for c0 in range(0, _OFF_LR, PROJ_PIECE):
            w_ref[:, c0:c0 + PROJ_PIECE] = w32_ref[c0:c0 + PROJ_PIECE, :].T.astype(_BF16)
        code_t = jnp.concatenate([w32_ref[_OFF_LR:D_IN_PROJ, :],
                                  jnp.zeros((LANES - GLA_LOW_RANK, w32_ref.shape[1]), _F32)], axis=0)
        wc_ref[...] = code_t.T.astype(_BF16)

    def start_tile(first_tile):
        if first_tile:
            k_hist[0:LEFT, :] = jnp.zeros((LEFT, D_ATT), _BF16)
            vt_hist[0:N_LEFT_BLOCKS] = jnp.zeros((N_LEFT_BLOCKS, D_ATT, KEY_BLOCK), _BF16)
            state_ref[...] = jnp.zeros_like(state_ref)
        else:
            k_hist[0:LEFT, :] = k_hist[seq_tile:seq_tile + LEFT, :]
            vt_hist[0:N_LEFT_BLOCKS] = vt_hist[n_new_blocks:n_new_blocks + N_LEFT_BLOCKS]
        k_hist[LEFT:LEFT + seq_tile, :] = nk_ref[...]
        for blk in range(n_new_blocks):
            vt_hist[N_LEFT_BLOCKS + blk] = nv_ref[blk * KEY_BLOCK:(blk + 1) * KEY_BLOCK, :].T
        for dst, src in ((q_ref, nq_ref), (ag_ref, nag_ref), (gqk_ref, ngqk_ref), (gv_ref, ngv_ref),
                         (gg_ref, ngg_ref), (loga_ref, nla_ref)):
            dst[...] = src[...]
        hb_ref[...] = nhb_ref[...]

    def norm_next(cp):
        slab = pl.ds(cp * PAIR_ROWS, PAIR_ROWS)
        nhb_ref[slab, :] = _layer_norm(x_ref[slab, :], g_ref[...], b_ref[...]).astype(_BF16)

    def proj_piece(i):
        if i == N_MAIN_PIECES:
            code_ref[...] = _dot(hb_ref[...], wc_ref[...]).astype(_BF16)
            return
        if i == N_MAIN_PIECES + 1:
            logit = _dot(code_ref[...], wg_ref[...]) + bg_ref[...]
            nla_ref[...] = jax.nn.log_sigmoid(logit) * (1.0 / GLA_TAU)
            return
        c0 = i * PROJ_PIECE
        acc = _dot(hb_ref[...], w_ref[:, c0:c0 + PROJ_PIECE])

        def put(ref, base, val):
            ref[:, c0 - base:c0 - base + PROJ_PIECE] = val.astype(ref.dtype)

        if c0 < D_ATT:
            put(nq_ref, 0, acc * (ATT_HEAD_DIM ** -0.5 * LOG2_E))
        elif c0 < 2 * D_ATT:
            put(nk_ref, D_ATT, acc)
        elif c0 < _OFF_AG:
            put(nv_ref, 2 * D_ATT, acc)
        elif c0 < _OFF_GQK:
            put(nag_ref, _OFF_AG, _silu(acc))
        elif c0 < _OFF_GQK + D_GLA_K:
            put(ngqk_ref, _OFF_GQK, acc * (GLA_HEAD_K ** -0.5))
        elif c0 < _OFF_GV:
            put(ngqk_ref, _OFF_GQK, acc)
        elif c0 < _OFF_GG:
            put(ngv_ref, _OFF_GV, acc)
        else:
            put(ngg_ref, _OFF_GG, _silu(acc) * gn_ref[:, c0 - _OFF_GG:c0 - _OFF_GG + PROJ_PIECE])

    lane = lax.broadcasted_iota(jnp.int32, (1, LANES), 1)
    low_half = lane < ATT_HEAD_DIM
    top_rows = lax.broadcasted_iota(jnp.int32, (LANES, 1), 0) < ATT_HEAD_DIM
    row_i = lax.broadcasted_iota(jnp.int32, (CHUNK, CHUNK), 0)
    col_i = lax.broadcasted_iota(jnp.int32, (CHUNK, CHUNK), 1)
    causal = row_i >= col_i
    chunk_row = lax.broadcasted_iota(jnp.int32, (CHUNK, 1), 0)
    ones_rows = jnp.ones((SUM_ROWS, KEY_SPAN), _BF16)

    def pair_body(cp, pieces, first_tile):
        r0 = cp * PAIR_ROWS
        rows = pl.ds(r0, PAIR_ROWS)
        skip = (N_LEFT_BLOCKS - cp) * KEY_BLOCK if first_tile else 0
        n_keys = KEY_SPAN - skip
        span = pl.ds(r0 + skip, n_keys)
        pieces = list(pieces)

        def proj_next():
            piece = pieces.pop(0)
            if piece is not None:
                proj_piece(piece)

        def att_scores(j):
            cols = slice(j * LANES, (j + 1) * LANES)
            qp = q_ref[rows, cols]
            zero = jnp.zeros_like(qp)
            q_blk = jnp.concatenate([jnp.where(low_half, qp, zero), jnp.where(low_half, zero, qp)], axis=0)
            kp = k_hist[span, cols]
            sc = lax.dot_general(kp, q_blk, _NT, preferred_element_type=_F32)
            slabs = []
            for lo, hi, bias_row in ((0, BIAS_TOP, 0), (BIAS_TOP, KEY_SPAN - BIAS_BOTTOM, None),
                                     (KEY_SPAN - BIAS_BOTTOM, KEY_SPAN, BIAS_TOP)):
                start = max(lo, skip)
                if start >= hi:
                    continue
                slab = sc[start - skip:hi - skip]
                if bias_row is not None:
                    slab = slab + bias_ref[j, bias_row + start - lo:bias_row + hi - lo, :]
                slabs.append(slab)
            return jnp.concatenate(slabs, axis=0)

        def att_softmax(sc):
            m = jnp.max(sc, axis=0, keepdims=True)
            return jnp.exp2(sc - m).astype(_BF16)

        def att_out(j, pe):
            cols = slice(j * LANES, (j + 1) * LANES)
            partial = []

            def key_tile(t):
                k0, k1 = t * PV_KEY_TILE, min((t + 1) * PV_KEY_TILE, n_keys)
                if k0 >= k1:
                    return
                first_block = cp + (skip + k0) // KEY_BLOCK
                vt = jnp.concatenate([vt_hist[first_block + i, cols, :] for i in range((k1 - k0) // KEY_BLOCK)],
                                     axis=1)
                partial.append(_dot(jnp.concatenate([vt, ones_rows[:, k0:k1]], axis=0), pe[k0:k1, :]))

            def finish():
                ot = functools.reduce(jnp.add, partial)
                inv = 1.0 / ot[LANES:LANES + 1, :]
                num = jnp.where(top_rows, ot[0:LANES, 0:LANES], ot[0:LANES, LANES:2 * LANES])
                scale = jnp.where(top_rows, inv[:, 0:LANES], inv[:, LANES:2 * LANES])
                att = (num * scale).T
                mix_ref[rows, cols] = (att * ag_ref[rows, cols]).astype(_BF16)

            return key_tile, finish

        def gla_cumsum(c):
            crow = pl.ds(cp * PAIR_ROWS + c * CHUNK, CHUNK)
            cum = loga_ref[crow, :]
            shift = 1
            while shift < CHUNK:
                cum = cum + jnp.where(chunk_row >= shift, pltpu.roll(cum, shift, axis=0), 0.0)
                shift *= 2
            return cum

        def gla_scores(c, cum):
            crow = pl.ds(cp * PAIR_ROWS + c * CHUNK, CHUNK)
            cum_end = cum[CHUNK - 1:CHUNK, :]
            e_pos = jnp.exp(cum)
            e_neg = jnp.exp(-cum)
            gq = gqk_ref[crow, 0:D_GLA_K]
            gk = gqk_ref[crow, D_GLA_K:2 * D_GLA_K]
            q_fwd = gq * e_pos
            q_bwd = gq * e_neg
            k_fwd = (gk * e_pos).astype(_BF16)
            k_bwd = (gk * e_neg).astype(_BF16)
            k_end = (gk * jnp.exp(cum_end - cum)).astype(_BF16)
            decay = jnp.exp(cum_end)
            parts = []
            for g in range(N_GLA_HEADS // 2):
                pcols = slice(g * LANES, (g + 1) * LANES)
                prows = slice(g * LANES, (g + 1) * LANES)
                state = state_ref[prows, :]
                zero = jnp.zeros_like(q_fwd[:, pcols])
                stack = lambda a: jnp.concatenate(
                    [jnp.where(low_half, a[:, pcols], zero), jnp.where(low_half, zero, a[:, pcols])],
                    axis=0).astype(_BF16)
                qf, qb = stack(q_fwd), stack(q_bwd)
                a_causal = lax.dot_general(qf, k_bwd[:, pcols], _NT, preferred_element_type=_F32)
                a_anti = lax.dot_general(qb, k_fwd[:, pcols], _NT, preferred_element_type=_F32)
                o_inter = _dot(qf, state.astype(_BF16))
                for e in range(2):
                    head_rows = slice(e * CHUNK, (e + 1) * CHUNK)
                    parts.append((a_causal[head_rows], a_anti[head_rows], o_inter[head_rows]))
                v_pair = gv_ref[crow, 2 * g * GLA_HEAD_V:(2 * g + 2) * GLA_HEAD_V]
                kv = lax.dot_general(k_end[:, pcols], v_pair, _TN, preferred_element_type=_F32)
                new_rows = jnp.where(top_rows, kv[:, 0:GLA_HEAD_V], kv[:, GLA_HEAD_V:])
                decay_rows = jnp.broadcast_to(decay[:, pcols], (LANES, LANES)).T
                state_ref[prows, :] = decay_rows * state + new_rows
            return parts

        def gla_out(c, parts):
            crow = pl.ds(cp * PAIR_ROWS + c * CHUNK, CHUNK)
            for h in range(N_GLA_HEADS):
                a_causal, a_anti, o_inter = parts[h]
                hv = slice(h * GLA_HEAD_V, (h + 1) * GLA_HEAD_V)
                att = jnp.where(causal, a_causal, a_anti).astype(_BF16)
                o = _dot(att, gv_ref[crow, hv]) + o_inter
                o = o * lax.rsqrt(jnp.mean(o * o, axis=-1, keepdims=True) + RMS_EPS)
                mix_ref[crow, D_ATT + h * GLA_HEAD_V:D_ATT + (h + 1) * GLA_HEAD_V] = (
                    o * gg_ref[crow, hv]).astype(_BF16)

        cum0 = gla_cumsum(0)
        cum1 = gla_cumsum(1)
        sc0 = att_scores(0)
        sc1 = att_scores(1)
        g0 = gla_scores(0, cum0)
        pv, finish = att_out(0, att_softmax(sc0))
        pv(0)
        proj_next()
        pv(1)
        sc2 = att_scores(2)
        pv(2)
        finish()
        gla_out(0, g0)
        g1 = gla_scores(1, cum1)
        pv, finish = att_out(1, att_softmax(sc1))
        pv(0)
        proj_next()
        pv(1)
        sc3 = att_scores(3)
        pv(2)
        finish()
        norm_next(cp)
        for j in (2, 3):
            pv, finish = att_out(j, att_softmax(sc2 if j == 2 else sc3))
            pv(0)
            proj_next()
            pv(1)
            if j == 2:
                gla_out(1, g1)
            pv(2)
            finish()
        assert not pieces

    n_pairs = seq_tile // PAIR_ROWS
    slots = PROJ_SLOTS_PER_PAIR * n_pairs
    heavy = [c // PROJ_PIECE for c in range(_OFF_AG, _OFF_GQK, PROJ_PIECE)] \
        + [c // PROJ_PIECE for c in range(_OFF_GG, _OFF_LR, PROJ_PIECE)] + [N_MAIN_PIECES + 1]
    light = [N_MAIN_PIECES] + [i for i in range(N_MAIN_PIECES) if i not in heavy]
    schedule = []
    for slot in range(slots):
        first = heavy if slot % PROJ_SLOTS_PER_PAIR == 0 else light
        source = first or heavy or light
        schedule.append(source.pop(0) if source else None)
    assert not heavy and not light

    def tile_body(first_tile):
        start_tile(first_tile)
        for cp in range(n_pairs):
            pair_body(cp, schedule[cp * PROJ_SLOTS_PER_PAIR:(cp + 1) * PROJ_SLOTS_PER_PAIR], first_tile)

    pl.when(s == 0)(functools.partial(tile_body, True))
    pl.when(s > 0)(functools.partial(tile_body, False))


def _epilogue_kernel(x_ref, g_ref, b_ref, mix_ref, p_ref, wo32_ref, wpg32_ref, bpg_ref, wp32_ref,
                     og_ref, ob_ref, out_ref, wo_ref, wpg_ref, wp_ref, *, alpha):
    @pl.when(pl.program_id(0) == 0)
    def _():
        for dst, src in ((wo_ref, wo32_ref), (wpg_ref, wpg32_ref), (wp_ref, wp32_ref)):
            dst[...] = src[...].astype(_BF16)

    half = x_ref.shape[0] // EPILOGUE_SPLIT
    halves = [pl.ds(i * half, half) for i in range(EPILOGUE_SPLIT)]

    def residual(rows):
        h = _layer_norm(x_ref[rows, :], g_ref[...], b_ref[...])
        return alpha * h + _dot(mix_ref[rows, :], wo_ref[...])

    def finish(rows, r, logit, ple):
        r = r + jax.nn.sigmoid(logit + bpg_ref[...]) * ple
        out_ref[rows, :] = _layer_norm(r, og_ref[...], ob_ref[...])

    ples = [_dot(p_ref[rows, :].astype(_BF16), wp_ref[...]) for rows in halves]
    rs = [residual(rows) for rows in halves]
    logits = [_dot(r.astype(_BF16), wpg_ref[...]) for r in rs]
    for rows, r, logit, ple in zip(halves, rs, logits, ples):
        finish(rows, r, logit, ple)


def _const_spec(shape):
    zeros = (0,) * len(shape)
    return pl.BlockSpec(shape, lambda *_: zeros, pipeline_mode=pl.Buffered(1))


def _rel_bias_span(rel_table):
    table = rel_table.astype(_F32)
    heads = table.shape[0]
    near = jnp.flip(table[:, REL_CLIP - (CHUNK - 1):2 * REL_CLIP], axis=1) - table[:, 2 * REL_CLIP:]
    ext = jnp.concatenate([jnp.zeros((heads, BAND - REL_CLIP), _F32), near], axis=1) * LOG2_E
    width = ext.shape[1] + 1
    tiled = jnp.tile(jnp.pad(ext, ((0, 0), (0, 1))), (1, CHUNK))[:, :CHUNK * (width - 1)]
    band = tiled.reshape(heads, CHUNK, width - 1)[:, :, CHUNK - 1:CHUNK - 1 + BAND]
    pad = lambda lo, hi: jnp.pad(band, ((0, 0), (0, 0), (lo, hi)), constant_values=MASK_VALUE)
    both = jnp.stack([pad(0, CHUNK), pad(CHUNK, 0)], axis=1)
    both = both.reshape(N_HEAD_PAIRS, 2, 2, CHUNK, KEY_SPAN)
    full = both.transpose(0, 4, 1, 2, 3).reshape(N_HEAD_PAIRS, KEY_SPAN, 2 * PAIR_ROWS)
    return jnp.concatenate([full[:, :BIAS_TOP], full[:, KEY_SPAN - BIAS_BOTTOM:]], axis=1)


def kernel(x, p, ln_in_g, ln_in_b, w_in, w_gla_gate, b_gla_gate, rel_bias,
           gla_norm_g, w_out, w_ple, w_ple_gate, b_ple_gate, ln_g, ln_b):
    batch, seq, d_model = x.shape
    depth = w_in.shape[0]
    assert depth == 1, "single-layer stack only"
    assert w_in.shape[2] == D_IN_PROJ
    assert seq % SEQ_TILE == 0 and SEQ_TILE % PAIR_ROWS == 0 and SEQ_TILE >= LEFT
    n = batch * seq
    assert n % ROW_TILE == 0
    d_ple = p.shape[-1]
    d_mix = D_ATT + D_GLA_V
    alpha = (2.0 * depth) ** 0.25

    x2 = x.reshape(n, d_model)
    p2 = p.reshape(n, d_ple)
    row = lambda v: v.reshape(1, -1).astype(_F32)
    wg_b = jnp.pad(w_gla_gate[0], ((0, LANES - GLA_LOW_RANK), (0, 0))).astype(_BF16)
    bias = _rel_bias_span(rel_bias[0])

    cparams = functools.partial(pltpu.CompilerParams, vmem_limit_bytes=VMEM_LIMIT)

    tiles_per_seq = seq // SEQ_TILE
    n_tiles = batch * tiles_per_seq
    tile_f32 = lambda width: pltpu.VMEM((SEQ_TILE, width), _F32)
    tile_bf16 = lambda width: pltpu.VMEM((SEQ_TILE, width), _BF16)
    projected = [tile_bf16(D_ATT), tile_f32(D_ATT), tile_f32(2 * D_GLA_K), tile_bf16(D_GLA_V),
                 tile_f32(D_GLA_V), tile_f32(D_GLA_K)]
    mix = pl.pallas_call(
        functools.partial(_proj_mix_kernel, seq_tile=SEQ_TILE, tiles_per_seq=tiles_per_seq),
        grid=(n_tiles + 1,),
        in_specs=[_const_spec((SEQ_TILE, d_model)),
                  pl.BlockSpec((SEQ_TILE, d_model), lambda k: (jnp.minimum(k + 1, n_tiles - 1), 0)),
                  _const_spec((1, d_model)), _const_spec((1, d_model)),
                  pl.BlockSpec((None, D_IN_PROJ, d_model), lambda k: (0, 0, 0), pipeline_mode=pl.Buffered(1)),
                  _const_spec((LANES, D_GLA_K)), _const_spec((1, D_GLA_K)), _const_spec((1, D_GLA_V)),
                  _const_spec((N_HEAD_PAIRS, BIAS_TOP + BIAS_BOTTOM, 2 * PAIR_ROWS))],
        out_specs=pl.BlockSpec((SEQ_TILE, d_mix), lambda k: (jnp.maximum(k - 1, 0), 0)),
        out_shape=jax.ShapeDtypeStruct((n, d_mix), _BF16),
        scratch_shapes=[pltpu.VMEM((d_model, _OFF_LR), _BF16),
                        pltpu.VMEM((d_model, LANES), _BF16),
                        tile_bf16(d_model), tile_bf16(d_model),
                        tile_bf16(LANES),
                        tile_bf16(D_ATT), tile_bf16(D_ATT), tile_bf16(D_ATT),
                        tile_f32(D_ATT), tile_f32(2 * D_GLA_K), tile_bf16(D_GLA_V),
                        tile_f32(D_GLA_V), tile_f32(D_GLA_K)]
                       + projected
                       + [pltpu.VMEM((LEFT + SEQ_TILE, D_ATT), _BF16),
                          pltpu.VMEM(((LEFT + SEQ_TILE) // KEY_BLOCK, D_ATT, KEY_BLOCK), _BF16),
                          pltpu.VMEM((D_GLA_K, GLA_HEAD_V), _F32)],
        compiler_params=cparams(dimension_semantics=("arbitrary",)),
        name="proj_mixers",
    )(x2, x2, row(ln_in_g), row(ln_in_b), jnp.swapaxes(w_in, 1, 2), wg_b, row(b_gla_gate[0]), row(gla_norm_g[0]), bias)

    row_spec = lambda width: pl.BlockSpec((ROW_TILE, width), lambda i: (i, 0))
    layer_weight = lambda rows, cols: pl.BlockSpec((None, rows, cols), lambda i: (0, 0, 0),
                                                   pipeline_mode=pl.Buffered(1))
    out = pl.pallas_call(
        functools.partial(_epilogue_kernel, alpha=alpha),
        grid=(n // ROW_TILE,),
        in_specs=[row_spec(d_model), _const_spec((1, d_model)), _const_spec((1, d_model)),
                  row_spec(d_mix), row_spec(d_ple),
                  layer_weight(d_mix, d_model), layer_weight(d_model, d_model),
                  _const_spec((1, d_model)), layer_weight(d_ple, d_model),
                  _const_spec((1, d_model)), _const_spec((1, d_model))],
        out_specs=row_spec(d_model),
        out_shape=jax.ShapeDtypeStruct((n, d_model), x.dtype),
        scratch_shapes=[pltpu.VMEM((d_mix, d_model), _BF16), pltpu.VMEM((d_model, d_model), _BF16),
                        pltpu.VMEM((d_ple, d_model), _BF16)],
        compiler_params=cparams(dimension_semantics=("arbitrary",)),
        name="out_proj_ple_norm",
    )(x2, row(ln_in_g), row(ln_in_b), mix, p2, w_out, w_ple_gate,
      row(b_ple_gate[0]), w_ple, row(ln_g[0]), row(ln_b[0]))
    return out.reshape(batch, seq, d_model)
```

```python
import functools
import math

import jax
import jax.numpy as jnp
from jax import lax
from jax.experimental import pallas as pl
from jax.experimental.pallas import tpu as pltpu

CHUNK = 64
N_ATT_HEADS = 8
ATT_HEAD_DIM = 64
D_ATT = N_ATT_HEADS * ATT_HEAD_DIM
LEFT_CHUNKS = 8
LEFT = LEFT_CHUNKS * CHUNK
BAND = LEFT + CHUNK
REL_CLIP = 128
N_GLA_HEADS = 4
GLA_HEAD_K = 64
GLA_HEAD_V = 128
D_GLA_K = N_GLA_HEADS * GLA_HEAD_K
D_GLA_V = N_GLA_HEADS * GLA_HEAD_V
GLA_LOW_RANK = 16
GLA_TAU = 16.0
LN_EPS = 1e-5
RMS_EPS = 1e-6
MASK_VALUE = -1e30
LOG2_E = math.log2(math.e)

LANES = 128
HEAD_PAIR = 2 * ATT_HEAD_DIM
assert HEAD_PAIR == LANES and 2 * GLA_HEAD_K == LANES
N_HEAD_PAIRS = N_ATT_HEADS // 2
PAIR_ROWS = 2 * CHUNK
KEY_SPAN = LEFT + PAIR_ROWS
KEY_BLOCK = LANES
N_SPAN_BLOCKS = KEY_SPAN // KEY_BLOCK
N_LEFT_BLOCKS = LEFT // KEY_BLOCK
assert PAIR_ROWS == KEY_BLOCK
BIAS_TOP = CHUNK
BIAS_BOTTOM = REL_CLIP + PAIR_ROWS
SUM_ROWS = 16
PV_KEY_TILE = 256
N_PV_TILES = -(-KEY_SPAN // PV_KEY_TILE)
assert N_PV_TILES == 3

_OFF_AG = 3 * D_ATT
_OFF_GQK = 4 * D_ATT
_OFF_GV = _OFF_GQK + 2 * D_GLA_K
_OFF_GG = _OFF_GV + D_GLA_V
_OFF_LR = _OFF_GG + D_GLA_V
D_IN_PROJ = _OFF_LR + GLA_LOW_RANK
PROJ_PIECE = 256
N_MAIN_PIECES = _OFF_LR // PROJ_PIECE
N_PROJ_PIECES = N_MAIN_PIECES + 2
PROJ_SLOTS_PER_PAIR = 4

ROW_TILE = 1024
EPILOGUE_SPLIT = 4
SEQ_TILE = 512
VMEM_LIMIT = 56 * 1024 * 1024

_F32 = jnp.float32
_BF16 = jnp.bfloat16
_NT = (((1,), (1,)), ((), ()))
_TN = (((0,), (0,)), ((), ()))


def _layer_norm(xf, g, b):
    mu = jnp.mean(xf, axis=-1, keepdims=True)
    xc = xf - mu
    var = jnp.mean(xc * xc, axis=-1, keepdims=True)
    return xc * lax.rsqrt(var + LN_EPS) * g + b


def _silu(x):
    return x * jax.nn.sigmoid(x)


def _dot(a, b):
    return jnp.dot(a, b, preferred_element_type=_F32)


def _proj_mix_kernel(x0_ref, x_ref, g_ref, b_ref, w32_ref, wg_ref, bg_ref, gn_ref, bias_ref,
                     mix_ref,
                     w_ref, wc_ref, hb_ref, nhb_ref, code_ref,
                     nq_ref, nk_ref, nv_ref, nag_ref, ngqk_ref, ngv_ref, ngg_ref, nla_ref,
                     q_ref, ag_ref, gqk_ref, gv_ref, gg_ref, loga_ref,
                     k_hist, vt_hist, state_ref, *, seq_tile, tiles_per_seq):
    k = pl.program_id(0)
    s = lax.rem(k + tiles_per_seq - 1, tiles_per_seq)
    n_new_blocks = seq_tile // KEY_BLOCK
    next_refs = (nq_ref, nk_ref, nv_ref, nag_ref, ngqk_ref, ngv_ref, ngg_ref, nla_ref)

    @pl.when(k == 0)
    def _():
        for ref in next_refs + (k_hist, vt_hist, state_ref):
            ref[...] = jnp.zeros_like(ref)
        nhb_ref[...] = _layer_norm(x0_ref[...], g_ref[...], b_ref[...]).astype(_BF16)
        for c0 in range(0, _OFF_LR, PROJ_PIECE):
            w_ref[:, c0:c0 + PROJ_PIECE] = w32_ref[c0:c0 + PROJ_PIECE, :].T.astype(_BF16)
        code_t = jnp.concatenate([w32_ref[_OFF_LR:D_IN_PROJ, :],
                                  jnp.zeros((LANES - GLA_LOW_RANK, w32_ref.shape[1]), _F32)], axis=0)
        wc_ref[...] = code_t.T.astype(_BF16)

    def start_tile(first_tile):
        if first_tile:
            k_hist[0:LEFT, :] = jnp.zeros((LEFT, D_ATT), _BF16)
            vt_hist[0:N_LEFT_BLOCKS] = jnp.zeros((N_LEFT_BLOCKS, D_ATT, KEY_BLOCK), _BF16)
            state_ref[...] = jnp.zeros_like(state_ref)
        else:
            k_hist[0:LEFT, :] = k_hist[seq_tile:seq_tile + LEFT, :]
            vt_hist[0:N_LEFT_BLOCKS] = vt_hist[n_new_blocks:n_new_blocks + N_LEFT_BLOCKS]
        k_hist[LEFT:LEFT + seq_tile, :] = nk_ref[...]
        for blk in range(n_new_blocks):
            vt_hist[N_LEFT_BLOCKS + blk] = nv_ref[blk * KEY_BLOCK:(blk + 1) * KEY_BLOCK, :].T
        for dst, src in ((q_ref, nq_ref), (ag_ref, nag_ref), (gqk_ref, ngqk_ref), (gv_ref, ngv_ref),
                         (gg_ref, ngg_ref), (loga_ref, nla_ref)):
            dst[...] = src[...]
        hb_ref[...] = nhb_ref[...]

    def norm_next(cp):
        slab = pl.ds(cp * PAIR_ROWS, PAIR_ROWS)
        nhb_ref[slab, :] = _layer_norm(x_ref[slab, :], g_ref[...], b_ref[...]).astype(_BF16)

    def proj_piece(i, hb_ref=hb_ref):
        if i == N_MAIN_PIECES:
            code_ref[...] = _dot(hb_ref[...], wc_ref[...]).astype(_BF16)
            return
        if i == N_MAIN_PIECES + 1:
            logit = _dot(code_ref[...], wg_ref[...]) + bg_ref[...]
            nla_ref[...] = jax.nn.log_sigmoid(logit) * (1.0 / GLA_TAU)
            return
        c0 = i * PROJ_PIECE
        acc = _dot(hb_ref[...], w_ref[:, c0:c0 + PROJ_PIECE])

        def put(ref, base, val):
            ref[:, c0 - base:c0 - base + PROJ_PIECE] = val.astype(ref.dtype)

        if c0 < D_ATT:
            put(nq_ref, 0, acc * (ATT_HEAD_DIM ** -0.5 * LOG2_E))
        elif c0 < 2 * D_ATT:
            put(nk_ref, D_ATT, acc)
        elif c0 < _OFF_AG:
            put(nv_ref, 2 * D_ATT, acc)
        elif c0 < _OFF_GQK:
            put(nag_ref, _OFF_AG, _silu(acc))
        elif c0 < _OFF_GQK + D_GLA_K:
            put(ngqk_ref, _OFF_GQK, acc * (GLA_HEAD_K ** -0.5))
        elif c0 < _OFF_GV:
            put(ngqk_ref, _OFF_GQK, acc)
        elif c0 < _OFF_GG:
            put(ngv_ref, _OFF_GV, acc)
        else:
            put(ngg_ref, _OFF_GG, _silu(acc) * gn_ref[:, c0 - _OFF_GG:c0 - _OFF_GG + PROJ_PIECE])

    lane = lax.broadcasted_iota(jnp.int32, (1, LANES), 1)
    low_half = lane < ATT_HEAD_DIM
    top_rows = lax.broadcasted_iota(jnp.int32, (LANES, 1), 0) < ATT_HEAD_DIM
    row_i = lax.broadcasted_iota(jnp.int32, (CHUNK, CHUNK), 0)
    col_i = lax.broadcasted_iota(jnp.int32, (CHUNK, CHUNK), 1)
    causal = row_i >= col_i
    chunk_row = lax.broadcasted_iota(jnp.int32, (CHUNK, 1), 0)
    ones_rows = jnp.ones((SUM_ROWS, KEY_SPAN), _BF16)

    def pair_body(cp, pieces, first_tile):
        r0 = cp * PAIR_ROWS
        rows = pl.ds(r0, PAIR_ROWS)
        skip = (N_LEFT_BLOCKS - cp) * KEY_BLOCK if first_tile else 0
        n_keys = KEY_SPAN - skip
        span = pl.ds(r0 + skip, n_keys)
        pieces = list(pieces)

        def proj_next():
            piece = pieces.pop(0)
            if piece is not None:
                proj_piece(piece)

        def att_scores(j):
            cols = slice(j * LANES, (j + 1) * LANES)
            qp = q_ref[rows, cols]
            zero = jnp.zeros_like(qp)
            q_blk = jnp.concatenate([jnp.where(low_half, qp, zero), jnp.where(low_half, zero, qp)], axis=0)
            kp = k_hist[span, cols]
            sc = lax.dot_general(kp, q_blk, _NT, preferred_element_type=_F32)
            slabs = []
            for lo, hi, bias_row in ((0, BIAS_TOP, 0), (BIAS_TOP, KEY_SPAN - BIAS_BOTTOM, None),
                                     (KEY_SPAN - BIAS_BOTTOM, KEY_SPAN, BIAS_TOP)):
                start = max(lo, skip)
                if start >= hi:
                    continue
                slab = sc[start - skip:hi - skip]
                if bias_row is not None:
                    slab = slab + bias_ref[j, bias_row + start - lo:bias_row + hi - lo, :]
                slabs.append(slab)
            return jnp.concatenate(slabs, axis=0)

        def att_softmax(sc):
            m = jnp.max(sc, axis=0, keepdims=True)
            return jnp.exp2(sc - m).astype(_BF16)

        def att_out(j, pe):
            cols = slice(j * LANES, (j + 1) * LANES)
            partial = []

            def key_tile(t):
                k0, k1 = t * PV_KEY_TILE, min((t + 1) * PV_KEY_TILE, n_keys)
                if k0 >= k1:
                    return
                first_block = cp + (skip + k0) // KEY_BLOCK
                vt = jnp.concatenate([vt_hist[first_block + i, cols, :] for i in range((k1 - k0) // KEY_BLOCK)],
                                     axis=1)
                partial.append(_dot(jnp.concatenate([vt, ones_rows[:, k0:k1]], axis=0), pe[k0:k1, :]))

            def finish():
                ot = functools.reduce(jnp.add, partial)
                inv = 1.0 / ot[LANES:LANES + 1, :]
                num = jnp.where(top_rows, ot[0:LANES, 0:LANES], ot[0:LANES, LANES:2 * LANES])
                scale = jnp.where(top_rows, inv[:, 0:LANES], inv[:, LANES:2 * LANES])
                att = (num * scale).T
                mix_ref[rows, cols] = (att * ag_ref[rows, cols]).astype(_BF16)

            return key_tile, finish

        def gla_cumsum(c):
            crow = pl.ds(cp * PAIR_ROWS + c * CHUNK, CHUNK)
            cum = loga_ref[crow, :]
            shift = 1
            while shift < CHUNK:
                cum = cum + jnp.where(chunk_row >= shift, pltpu.roll(cum, shift, axis=0), 0.0)
                shift *= 2
            return cum

        def gla_scores(c, cum):
            crow = pl.ds(cp * PAIR_ROWS + c * CHUNK, CHUNK)
            cum_end = cum[CHUNK - 1:CHUNK, :]
            e_pos = jnp.exp(cum)
            e_neg = jnp.exp(-cum)
            gq = gqk_ref[crow, 0:D_GLA_K]
            gk = gqk_ref[crow, D_GLA_K:2 * D_GLA_K]
            q_fwd = gq * e_pos
            q_bwd = gq * e_neg
            k_fwd = (gk * e_pos).astype(_BF16)
            k_bwd = (gk * e_neg).astype(_BF16)
            k_end = (gk * jnp.exp(cum_end - cum)).astype(_BF16)
            decay = jnp.exp(cum_end)
            parts = []
            for g in range(N_GLA_HEADS // 2):
                pcols = slice(g * LANES, (g + 1) * LANES)
                prows = slice(g * LANES, (g + 1) * LANES)
                state = state_ref[prows, :]
                zero = jnp.zeros_like(q_fwd[:, pcols])
                stack = lambda a: jnp.concatenate(
                    [jnp.where(low_half, a[:, pcols], zero), jnp.where(low_half, zero, a[:, pcols])],
                    axis=0).astype(_BF16)
                qf, qb = stack(q_fwd), stack(q_bwd)
                a_causal = lax.dot_general(qf, k_bwd[:, pcols], _NT, preferred_element_type=_F32)
                a_anti = lax.dot_general(qb, k_fwd[:, pcols], _NT, preferred_element_type=_F32)
                o_inter = _dot(qf, state.astype(_BF16))
                for e in range(2):
                    head_rows = slice(e * CHUNK, (e + 1) * CHUNK)
                    parts.append((a_causal[head_rows], a_anti[head_rows], o_inter[head_rows]))
                v_pair = gv_ref[crow, 2 * g * GLA_HEAD_V:(2 * g + 2) * GLA_HEAD_V]
                kv = lax.dot_general(k_end[:, pcols], v_pair, _TN, preferred_element_type=_F32)
                new_rows = jnp.where(top_rows, kv[:, 0:GLA_HEAD_V], kv[:, GLA_HEAD_V:])
                decay_rows = jnp.broadcast_to(decay[:, pcols], (LANES, LANES)).T
                state_ref[prows, :] = decay_rows * state + new_rows
            return parts

        def gla_out(c, parts):
            crow = pl.ds(cp * PAIR_ROWS + c * CHUNK, CHUNK)
            for h in range(N_GLA_HEADS):
                a_causal, a_anti, o_inter = parts[h]
                hv = slice(h * GLA_HEAD_V, (h + 1) * GLA_HEAD_V)
                att = jnp.where(causal, a_causal, a_anti).astype(_BF16)
                o = _dot(att, gv_ref[crow, hv]) + o_inter
                o = o * lax.rsqrt(jnp.mean(o * o, axis=-1, keepdims=True) + RMS_EPS)
                mix_ref[crow, D_ATT + h * GLA_HEAD_V:D_ATT + (h + 1) * GLA_HEAD_V] = (
                    o * gg_ref[crow, hv]).astype(_BF16)

        cum0 = gla_cumsum(0)
        cum1 = gla_cumsum(1)
        sc0 = att_scores(0)
        sc1 = att_scores(1)
        g0 = gla_scores(0, cum0)
        pv, finish = att_out(0, att_softmax(sc0))
        pv(0)
        proj_next()
        pv(1)
        sc2 = att_scores(2)
        pv(2)
        finish()
        gla_out(0, g0)
        g1 = gla_scores(1, cum1)
        pv, finish = att_out(1, att_softmax(sc1))
        pv(0)
        proj_next()
        pv(1)
        sc3 = att_scores(3)
        pv(2)
        finish()
        norm_next(cp)
        for j in (2, 3):
            pv, finish = att_out(j, att_softmax(sc2 if j == 2 else sc3))
            pv(0)
            proj_next()
            pv(1)
            if j == 2:
                gla_out(1, g1)
            pv(2)
            finish()
        assert not pieces

    n_pairs = seq_tile // PAIR_ROWS
    slots = PROJ_SLOTS_PER_PAIR * n_pairs
    heavy = [c // PROJ_PIECE for c in range(_OFF_AG, _OFF_GQK, PROJ_PIECE)] \
        + [c // PROJ_PIECE for c in range(_OFF_GG, _OFF_LR, PROJ_PIECE)] + [N_MAIN_PIECES + 1]
    light = [N_MAIN_PIECES] + [i for i in range(N_MAIN_PIECES) if i not in heavy]
    schedule = []
    for slot in range(slots):
        first = heavy if slot % PROJ_SLOTS_PER_PAIR == 0 else light
        source = first or heavy or light
        schedule.append(source.pop(0) if source else None)
    assert not heavy and not light

    def tile_body(first_tile):
        start_tile(first_tile)
        proj_piece(schedule[0], hb_ref=nhb_ref)
        for cp in range(n_pairs):
            pieces = schedule[cp * PROJ_SLOTS_PER_PAIR:(cp + 1) * PROJ_SLOTS_PER_PAIR]
            pair_body(cp, [None] + pieces[1:] if cp == 0 else pieces, first_tile)

    pl.when(s == 0)(functools.partial(tile_body, True))
    pl.when(s > 0)(functools.partial(tile_body, False))


def _epilogue_kernel(x_ref, g_ref, b_ref, mix_ref, p_ref, wo32_ref, wpg32_ref, bpg_ref, wp32_ref,
                     og_ref, ob_ref, out_ref, wo_ref, wpg_ref, wp_ref, *, alpha):
    @pl.when(pl.program_id(0) == 0)
    def _():
        for dst, src in ((wo_ref, wo32_ref), (wpg_ref, wpg32_ref), (wp_ref, wp32_ref)):
            dst[...] = src[...].astype(_BF16)

    half = x_ref.shape[0] // EPILOGUE_SPLIT
    halves = [pl.ds(i * half, half) for i in range(EPILOGUE_SPLIT)]

    def residual(rows):
        h = _layer_norm(x_ref[rows, :], g_ref[...], b_ref[...])
        return alpha * h + _dot(mix_ref[rows, :], wo_ref[...])

    def finish(rows, r, logit, ple):
        r = r + jax.nn.sigmoid(logit + bpg_ref[...]) * ple
        out_ref[rows, :] = _layer_norm(r, og_ref[...], ob_ref[...])

    ples = [_dot(p_ref[rows, :].astype(_BF16), wp_ref[...]) for rows in halves]
    rs = [residual(rows) for rows in halves]
    logits = [_dot(r.astype(_BF16), wpg_ref[...]) for r in rs]
    for rows, r, logit, ple in zip(halves, rs, logits, ples):
        finish(rows, r, logit, ple)


def _const_spec(shape):
    zeros = (0,) * len(shape)
    return pl.BlockSpec(shape, lambda *_: zeros, pipeline_mode=pl.Buffered(1))


def _rel_bias_span(rel_table):
    table = rel_table.astype(_F32)
    heads = table.shape[0]
    near = jnp.flip(table[:, REL_CLIP - (CHUNK - 1):2 * REL_CLIP], axis=1) - table[:, 2 * REL_CLIP:]
    ext = jnp.concatenate([jnp.zeros((heads, BAND - REL_CLIP), _F32), near], axis=1) * LOG2_E
    width = ext.shape[1] + 1
    tiled = jnp.tile(jnp.pad(ext, ((0, 0), (0, 1))), (1, CHUNK))[:, :CHUNK * (width - 1)]
    band = tiled.reshape(heads, CHUNK, width - 1)[:, :, CHUNK - 1:CHUNK - 1 + BAND]
    pad = lambda lo, hi: jnp.pad(band, ((0, 0), (0, 0), (lo, hi)), constant_values=MASK_VALUE)
    both = jnp.stack([pad(0, CHUNK), pad(CHUNK, 0)], axis=1)
    both = both.reshape(N_HEAD_PAIRS, 2, 2, CHUNK, KEY_SPAN)
    full = both.transpose(0, 4, 1, 2, 3).reshape(N_HEAD_PAIRS, KEY_SPAN, 2 * PAIR_ROWS)
    return jnp.concatenate([full[:, :BIAS_TOP], full[:, KEY_SPAN - BIAS_BOTTOM:]], axis=1)


def kernel(x, p, ln_in_g, ln_in_b, w_in, w_gla_gate, b_gla_gate, rel_bias,
           gla_norm_g, w_out, w_ple, w_ple_gate, b_ple_gate, ln_g, ln_b):
    batch, seq, d_model = x.shape
    depth = w_in.shape[0]
    assert depth == 1, "single-layer stack only"
    assert w_in.shape[2] == D_IN_PROJ
    assert seq % SEQ_TILE == 0 and SEQ_TILE % PAIR_ROWS == 0 and SEQ_TILE >= LEFT
    n = batch * seq
    assert n % ROW_TILE == 0
    d_ple = p.shape[-1]
    d_mix = D_ATT + D_GLA_V
    alpha = (2.0 * depth) ** 0.25

    x2 = x.reshape(n, d_model)
    p2 = p.reshape(n, d_ple)
    row = lambda v: v.reshape(1, -1).astype(_F32)
    wg_b = jnp.pad(w_gla_gate[0], ((0, LANES - GLA_LOW_RANK), (0, 0))).astype(_BF16)
    bias = _rel_bias_span(rel_bias[0])

    cparams = functools.partial(pltpu.CompilerParams, vmem_limit_bytes=VMEM_LIMIT)

    tiles_per_seq = seq // SEQ_TILE
    n_tiles = batch * tiles_per_seq
    tile_f32 = lambda width: pltpu.VMEM((SEQ_TILE, width), _F32)
    tile_bf16 = lambda width: pltpu.VMEM((SEQ_TILE, width), _BF16)
    projected = [tile_bf16(D_ATT), tile_f32(D_ATT), tile_f32(2 * D_GLA_K), tile_bf16(D_GLA_V),
                 tile_f32(D_GLA_V), tile_f32(D_GLA_K)]
    mix = pl.pallas_call(
        functools.partial(_proj_mix_kernel, seq_tile=SEQ_TILE, tiles_per_seq=tiles_per_seq),
        grid=(n_tiles + 1,),
        in_specs=[_const_spec((SEQ_TILE, d_model)),
                  pl.BlockSpec((SEQ_TILE, d_model), lambda k: (jnp.minimum(k + 1, n_tiles - 1), 0)),
                  _const_spec((1, d_model)), _const_spec((1, d_model)),
                  pl.BlockSpec((None, D_IN_PROJ, d_model), lambda k: (0, 0, 0), pipeline_mode=pl.Buffered(1)),
                  _const_spec((LANES, D_GLA_K)), _const_spec((1, D_GLA_K)), _const_spec((1, D_GLA_V)),
                  _const_spec((N_HEAD_PAIRS, BIAS_TOP + BIAS_BOTTOM, 2 * PAIR_ROWS))],
        out_specs=pl.BlockSpec((SEQ_TILE, d_mix), lambda k: (jnp.maximum(k - 1, 0), 0)),
        out_shape=jax.ShapeDtypeStruct((n, d_mix), _BF16),
        scratch_shapes=[pltpu.VMEM((d_model, _OFF_LR), _BF16),
                        pltpu.VMEM((d_model, LANES), _BF16),
                        tile_bf16(d_model), tile_bf16(d_model),
                        tile_bf16(LANES),
                        tile_bf16(D_ATT), tile_bf16(D_ATT), tile_bf16(D_ATT),
                        tile_f32(D_ATT), tile_f32(2 * D_GLA_K), tile_bf16(D_GLA_V),
                        tile_f32(D_GLA_V), tile_f32(D_GLA_K)]
                       + projected
                       + [pltpu.VMEM((LEFT + SEQ_TILE, D_ATT), _BF16),
                          pltpu.VMEM(((LEFT + SEQ_TILE) // KEY_BLOCK, D_ATT, KEY_BLOCK), _BF16),
                          pltpu.VMEM((D_GLA_K, GLA_HEAD_V), _F32)],
        compiler_params=cparams(dimension_semantics=("arbitrary",)),
        name="proj_mixers",
    )(x2, x2, row(ln_in_g), row(ln_in_b), jnp.swapaxes(w_in, 1, 2), wg_b, row(b_gla_gate[0]), row(gla_norm_g[0]), bias)

    row_spec = lambda width: pl.BlockSpec((ROW_TILE, width), lambda i: (i, 0))
    layer_weight = lambda rows, cols: pl.BlockSpec((None, rows, cols), lambda i: (0, 0, 0),
                                                   pipeline_mode=pl.Buffered(1))
    out = pl.pallas_call(
        functools.partial(_epilogue_kernel, alpha=alpha),
        grid=(n // ROW_TILE,),
        in_specs=[row_spec(d_model), _const_spec((1, d_model)), _const_spec((1, d_model)),
                  row_spec(d_mix), row_spec(d_ple),
                  layer_weight(d_mix, d_model), layer_weight(d_model, d_model),
                  _const_spec((1, d_model)), layer_weight(d_ple, d_model),
                  _const_spec((1, d_model)), _const_spec((1, d_model))],
        out_specs=row_spec(d_model),
        out_shape=jax.ShapeDtypeStruct((n, d_model), x.dtype),
        scratch_shapes=[pltpu.VMEM((d_mix, d_model), _BF16), pltpu.VMEM((d_model, d_model), _BF16),
                        pltpu.VMEM((d_ple, d_model), _BF16)],
        compiler_params=cparams(dimension_semantics=("arbitrary",)),
        name="out_proj_ple_norm",
    )(x2, row(ln_in_g), row(ln_in_b), mix, p2, w_out, w_ple_gate,
      row(b_ple_gate[0]), w_ple, row(ln_g[0]), row(ln_b[0]))
    return out.reshape(batch, seq, d_model)
```
